```python
import math
import jax
import jax.numpy as jnp
from jax import lax
import numpy as np


D_MODEL = 2048
BATCH = 8
SEQ = 4096
DEPTH = 2

GRID_W = 64
EPS = 1e-6
A_HEADS = 8
A_DK = 128
A_DV = 128
A_CHUNK = 64
CONV_W = 5
B_HEADS = 8
B_KV_HEADS = 2
B_HD = 128
Q_BLOCK = 128
ROPE_THETA = 10000.0
C_WIDTH = 2048
C_GROUPS = 16
C_CHUNK = 128

A_W = A_HEADS * A_DK
B_W = B_HEADS * B_HD
B_KV_W = B_KV_HEADS * B_HD
IN0_SIZES = (A_W, A_W, A_HEADS * A_DV, A_HEADS * A_DV, 4 * A_HEADS, B_W, B_KV_W, B_KV_W, B_W)

kernel_name = "hybrid_deltanet_gqa_gmlp_encoder"


def rms_norm(x, g):
    xf = x.astype(jnp.float32)
    y = xf * lax.rsqrt(jnp.mean(xf * xf, axis=-1, keepdims=True) + EPS)
    return (y * g.astype(jnp.float32)).astype(x.dtype)


def layer_norm(x, g, b):
    xf = x.astype(jnp.float32)
    mu = jnp.mean(xf, axis=-1, keepdims=True)
    var = jnp.mean(jnp.square(xf - mu), axis=-1, keepdims=True)
    y = (xf - mu) * lax.rsqrt(var + EPS) * g.astype(jnp.float32) + b.astype(jnp.float32)
    return y.astype(x.dtype)


def l2norm(t):
    tf = t.astype(jnp.float32)
    return tf * lax.rsqrt(jnp.sum(tf * tf, axis=-1, keepdims=True) + EPS)


def split_cols(t, sizes):
    offs = []
    acc = 0
    for s in sizes[:-1]:
        acc += s
        offs.append(acc)
    return jnp.split(t, offs, axis=-1)


def centred_dwconv(x, w):
    pad = w.shape[0] // 2
    return lax.conv_general_dilated(
        x, w[:, None, :].astype(x.dtype), window_strides=(1,), padding=[(pad, pad)],
        dimension_numbers=("NWC", "WIO", "NWC"), feature_group_count=x.shape[-1])


def gated_delta_chunked(q, k, v, g, beta):
    f32 = jnp.float32
    bsz, nh, L, dk = q.shape
    dv = v.shape[-1]
    C = A_CHUNK
    n = L // C
    q = (q.astype(f32) * (dk ** -0.5)).reshape(bsz, nh, n, C, dk)
    k = k.astype(f32).reshape(bsz, nh, n, C, dk)
    v = v.astype(f32).reshape(bsz, nh, n, C, dv)
    g = jnp.cumsum(g.astype(f32).reshape(bsz, nh, n, C), axis=-1)
    beta = beta.astype(f32).reshape(bsz, nh, n, C)
    idx = jnp.arange(C)
    incl = idx[:, None] >= idx[None, :]
    strict = idx[:, None] > idx[None, :]
    decay = jnp.exp(jnp.where(incl, g[..., :, None] - g[..., None, :], -jnp.inf))
    k_beta = k * beta[..., None]
    m = jnp.where(strict, jnp.einsum('bhnid,bhnjd->bhnij', k_beta, k) * decay, 0.0)
    eye = jnp.eye(C, dtype=f32)
    t_inv = lax.linalg.triangular_solve(eye + m, jnp.broadcast_to(eye, m.shape),
                                        left_side=True, lower=True, unit_diagonal=True)
    u = jnp.einsum('bhnij,bhnje->bhnie', t_inv, v * beta[..., None])
    w = jnp.einsum('bhnij,bhnjd->bhnid', t_inv, k_beta * jnp.exp(g)[..., None])
    attn = jnp.einsum('bhnid,bhnjd->bhnij', q, k) * decay
    q_dec = q * jnp.exp(g)[..., None]
    k_dec = k * jnp.exp(g[..., -1:] - g)[..., None]
    g_last = jnp.exp(g[..., -1])
    xs = tuple(jnp.moveaxis(t, 2, 0) for t in (q_dec, k_dec, u, w, attn, g_last))

    def step(S, inp):
        q_c, k_c, u_c, w_c, a_c, gl_c = inp
        v_new = u_c - jnp.einsum('bhcd,bhde->bhce', w_c, S)
        o_c = jnp.einsum('bhcd,bhde->bhce', q_c, S) + jnp.einsum('bhij,bhje->bhie', a_c, v_new)
        S = S * gl_c[..., None, None] + jnp.einsum('bhcd,bhce->bhde', k_c, v_new)
        return S, o_c

    S0 = jnp.zeros((bsz, nh, dk, dv), f32)
    _, o = lax.scan(step, S0, xs)
    return jnp.moveaxis(o, 0, 2).reshape(bsz, nh, L, dv)


def mixer_a(aq, ak, av, agate, alog_in, conv_w, a_log, dt_bias, onorm_g):
    bsz, L, _ = aq.shape
    qkv = jax.nn.silu(centred_dwconv(jnp.concatenate([aq, ak, av], axis=-1), conv_w))
    q, k, v = split_cols(qkv, (A_W, A_W, A_HEADS * A_DV))
    q = l2norm(jnp.transpose(q.reshape(bsz, L, A_HEADS, A_DK), (0, 2, 1, 3)))
    k = l2norm(jnp.transpose(k.reshape(bsz, L, A_HEADS, A_DK), (0, 2, 1, 3)))
    v = jnp.transpose(v.reshape(bsz, L, A_HEADS, A_DV), (0, 2, 1, 3))
    a_f, a_b, b_f, b_b = jnp.split(alog_in.astype(jnp.float32), 4, axis=-1)

    def log_decay(a, d):
        gd = -jnp.exp(a_log[d].astype(jnp.float32)) * jax.nn.softplus(a + dt_bias[d].astype(jnp.float32))
        return jnp.transpose(gd, (0, 2, 1))

    def flip(t):
        return jnp.flip(t, axis=2)

    o_f = gated_delta_chunked(q, k, v, log_decay(a_f, 0), jnp.transpose(jax.nn.sigmoid(b_f), (0, 2, 1)))
    o_b = flip(gated_delta_chunked(flip(q), flip(k), flip(v), flip(log_decay(a_b, 1)),
                                   flip(jnp.transpose(jax.nn.sigmoid(b_b), (0, 2, 1)))))
    o = jnp.transpose(o_f + o_b, (0, 2, 1, 3))
    o = rms_norm(o, onorm_g).reshape(bsz, L, A_HEADS * A_DV)
    return (o * jax.nn.silu(agate.astype(jnp.float32))).astype(aq.dtype)


def axial_rope_tables(L):
    rows = L // GRID_W
    t = jnp.arange(rows * GRID_W)
    row = (t // GRID_W).astype(jnp.float32)
    col = (t % GRID_W).astype(jnp.float32)
    n_freq = B_HD // 4
    inv = ROPE_THETA ** (-jnp.arange(n_freq, dtype=jnp.float32) / n_freq)
    ang = jnp.concatenate([row[:, None] * inv, col[:, None] * inv], axis=-1)
    return jnp.cos(ang), jnp.sin(ang)


def apply_axial_rope(x, cos, sin):
    xf = x.astype(jnp.float32)
    nf = B_HD // 4
    half = B_HD // 2
    c = cos[None, :, None, :]
    s = sin[None, :, None, :]

    def rot(xh, cc, ss):
        x1 = xh[..., :nf]
        x2 = xh[..., nf:]
        return jnp.concatenate([x1 * cc - x2 * ss, x2 * cc + x1 * ss], axis=-1)

    out = jnp.concatenate([rot(xf[..., :half], c[..., :nf], s[..., :nf]),
                           rot(xf[..., half:], c[..., nf:], s[..., nf:])], axis=-1)
    return out.astype(x.dtype)


def mixer_b(bq, bk, bv, bgate, qn_g, kn_g):
    bsz, L, _ = bq.shape
    grp = B_HEADS // B_KV_HEADS
    q = rms_norm(bq.reshape(bsz, L, B_HEADS, B_HD), qn_g)
    k = rms_norm(bk.reshape(bsz, L, B_KV_HEADS, B_HD), kn_g)
    v = bv.reshape(bsz, L, B_KV_HEADS, B_HD)
    cos, sin = axial_rope_tables(L)
    q = apply_axial_rope(q, cos, sin)
    k = apply_axial_rope(k, cos, sin)
    q = jnp.moveaxis(q.reshape(bsz, L // Q_BLOCK, Q_BLOCK, B_KV_HEADS, grp, B_HD), 1, 0)
    scale = B_HD ** -0.5

    def attend(qb):
        s = jnp.einsum('bqkgd,bskd->bkgqs', qb, k, preferred_element_type=jnp.float32) * scale
        p = jax.nn.softmax(s, axis=-1)
        return jnp.einsum('bkgqs,bskd->bqkgd', p.astype(v.dtype), v)

    o = lax.map(attend, q)
    o = jnp.moveaxis(o, 0, 1).reshape(bsz, L, B_W)
    return o * jax.nn.silu(bgate)


def hybrid_layer(x, norm_g, w_in, conv_w, a_log, dt_bias, a_onorm_g, qn_g, kn_g, w_out):
    h = rms_norm(x, norm_g)
    aq, ak, av, agate, alog_in, bq, bk, bv, bgate = split_cols(h @ w_in, IN0_SIZES)
    ya = mixer_a(aq, ak, av, agate, alog_in, conv_w, a_log, dt_bias, a_onorm_g)
    yb = mixer_b(bq, bk, bv, bgate, qn_g, kn_g)
    return x + jnp.concatenate([ya, yb.astype(ya.dtype)], axis=-1) @ w_out


def gmlp_layer(x, norm_g, w_in, ln_g, ln_b, w_s, b_s, w_out):
    bsz, L, _ = x.shape
    h = rms_norm(x, norm_g)
    u, v, gate = jnp.split(h @ w_in, 3, axis=-1)
    u = jax.nn.gelu(u, approximate=False)
    v = layer_norm(jax.nn.gelu(v, approximate=False), ln_g, ln_b)
    n = L // C_CHUNK
    cg = C_WIDTH // C_GROUPS
    vc = v.reshape(bsz, n, C_CHUNK, C_GROUPS, cg)
    s = jnp.einsum('gts,bnsgc->bntgc', w_s, vc) + jnp.transpose(b_s)[None, None, :, :, None]
    z = u * s.reshape(bsz, L, C_WIDTH) * jax.nn.silu(gate)
    return x + z @ w_out


def setup_inputs(seed: int = 0) -> dict:
    key = jax.random.key(seed)
    ks = jax.random.split(key, 20)
    ne = (DEPTH + 1) // 2
    no = DEPTH // 2
    f32 = jnp.float32

    def nrm(k, shape, scale):
        return jax.random.normal(k, shape, f32) * scale

    in0 = sum(IN0_SIZES)
    x = nrm(ks[0], (BATCH, SEQ, D_MODEL), 1.0)
    norm0_g = 1.0 + nrm(ks[1], (ne, D_MODEL), 0.02)
    w_in0 = nrm(ks[2], (ne, D_MODEL, in0), D_MODEL ** -0.5)
    conv0_w = nrm(ks[3], (ne, CONV_W, 3 * A_W), CONV_W ** -0.5)
    a_log0 = jnp.log(jax.random.uniform(ks[4], (ne, 2, A_HEADS), f32, 1.0, 16.0))
    dt = jnp.exp(jax.random.uniform(ks[5], (ne, 2, A_HEADS), f32, math.log(1e-3), math.log(1e-1)))
    dt_bias0 = dt + jnp.log(-jnp.expm1(-dt))
    a_onorm_g0 = 1.0 + nrm(ks[6], (ne, A_DV), 0.02)
    b_qnorm_g0 = 1.0 + nrm(ks[7], (ne, B_HD), 0.02)
    b_knorm_g0 = 1.0 + nrm(ks[8], (ne, B_HD), 0.02)
    w_out0 = nrm(ks[9], (ne, A_HEADS * A_DV + B_W, D_MODEL), (A_HEADS * A_DV + B_W) ** -0.5)
    norm1_g = 1.0 + nrm(ks[10], (no, D_MODEL), 0.02)
    w_in1 = nrm(ks[11], (no, D_MODEL, 3 * C_WIDTH), D_MODEL ** -0.5)
    c_ln_g1 = 1.0 + nrm(ks[12], (no, C_WIDTH), 0.02)
    c_ln_b1 = nrm(ks[13], (no, C_WIDTH), 0.02)
    c_ws1 = nrm(ks[14], (no, C_GROUPS, C_CHUNK, C_CHUNK), C_CHUNK ** -0.5)
    c_bs1 = 1.0 + nrm(ks[15], (no, C_GROUPS, C_CHUNK), 0.01)
    w_out1 = nrm(ks[16], (no, C_WIDTH, D_MODEL), C_WIDTH ** -0.5)
    return {"x": x, "norm0_g": norm0_g, "w_in0": w_in0, "conv0_w": conv0_w, "a_log0": a_log0,
            "dt_bias0": dt_bias0, "a_onorm_g0": a_onorm_g0, "b_qnorm_g0": b_qnorm_g0,
            "b_knorm_g0": b_knorm_g0, "w_out0": w_out0, "norm1_g": norm1_g, "w_in1": w_in1,
            "c_ln_g1": c_ln_g1, "c_ln_b1": c_ln_b1, "c_ws1": c_ws1, "c_bs1": c_bs1, "w_out1": w_out1}


def reference(x, norm0_g, w_in0, conv0_w, a_log0, dt_bias0, a_onorm_g0, b_qnorm_g0, b_knorm_g0, w_out0,
              norm1_g, w_in1, c_ln_g1, c_ln_b1, c_ws1, c_bs1, w_out1):
    for layer in range(DEPTH):
        i = layer // 2
        if layer % 2 == 0:
            x = hybrid_layer(x, norm0_g[i], w_in0[i], conv0_w[i], a_log0[i], dt_bias0[i],
                             a_onorm_g0[i], b_qnorm_g0[i], b_knorm_g0[i], w_out0[i])
        else:
            x = gmlp_layer(x, norm1_g[i], w_in1[i], c_ln_g1[i], c_ln_b1[i], c_ws1[i], c_bs1[i], w_out1[i])
    return x
```

```python
import functools
import math

import jax
import jax.numpy as jnp
from jax import lax
from jax.experimental import pallas as pl
from jax.experimental.pallas import tpu as pltpu

F32 = jnp.float32
BF16 = jnp.bfloat16

EPS = 1e-6
GRID_W = 64
ROPE_THETA = 10000.0
A_HEADS = 8
A_DK = 128
A_DV = 128
A_CHUNK = 64
CONV_W = 5
A_W = A_HEADS * A_DK
B_HEADS = 8
B_KV_HEADS = 2
B_HD = 128
B_GROUP = B_HEADS // B_KV_HEADS
B_W = B_HEADS * B_HD
B_KV_W = B_KV_HEADS * B_HD
C_GROUPS = 16
C_CHUNK = 128

LANES = 128
VMEM_LIMIT = 56 * 1024 * 1024

COL_AQ = 0
COL_AK = A_W
COL_AV = 2 * A_W
COL_AGATE = 3 * A_W
COL_BQ = 4 * A_W
COL_BGATE = COL_BQ + B_W
COL_BK = COL_BGATE + B_W
COL_BV = COL_BK + B_KV_W
P0_COLS = COL_BV + B_KV_W


def _params(sem):
    return pltpu.CompilerParams(dimension_semantics=sem, vmem_limit_bytes=VMEM_LIMIT)


def _dot(a, b):
    return jnp.dot(a, b, preferred_element_type=F32)


def _dot_nt(a, b):
    return lax.dot_general(a, b, (((1,), (1,)), ((), ())), preferred_element_type=F32)


def _rmsnorm_rows(x_ref, g_ref, h_ref, rows):
    def body(r, carry):
        sl = pl.ds(pl.multiple_of(r * rows, rows), rows)
        x = x_ref[sl, :]
        ms = jnp.mean(x * x, axis=-1, keepdims=True)
        h_ref[sl, :] = (x * lax.rsqrt(ms + EPS) * g_ref[...]).astype(BF16)
        return carry
    lax.fori_loop(0, x_ref.shape[0] // rows, body, 0)


def _proj0_kernel(x_ref, g_ref, w_ref, wl_ref, p_ref, lg_ref, h_ref):
    @pl.when(pl.program_id(1) == 0)
    def _():
        _rmsnorm_rows(x_ref, g_ref, h_ref, 128)
        lg_ref[...] = _dot(h_ref[...], wl_ref[...])

    p_ref[...] = _dot(h_ref[...], w_ref[...]).astype(BF16)


def _proj0(x2, g, w_main, w_log, tm, tn):
    t, d = x2.shape
    n = w_main.shape[1]
    return pl.pallas_call(
        _proj0_kernel,
        grid=(t // tm, n // tn),
        in_specs=[
            pl.BlockSpec((tm, d), lambda i, j: (i, 0)),
            pl.BlockSpec((1, d), lambda i, j: (0, 0)),
            pl.BlockSpec((d, tn), lambda i, j: (0, j)),
            pl.BlockSpec((d, LANES), lambda i, j: (0, 0)),
        ],
        out_specs=[
            pl.BlockSpec((tm, tn), lambda i, j: (i, j)),
            pl.BlockSpec((tm, LANES), lambda i, j: (i, 0)),
        ],
        out_shape=[
            jax.ShapeDtypeStruct((t, n), BF16),
            jax.ShapeDtypeStruct((t, LANES), F32),
        ],
        scratch_shapes=[pltpu.VMEM((tm, d), BF16)],
        compiler_params=_params(("arbitrary", "arbitrary")),
    )(x2, g, w_main, w_log)


HALO = 16


def _softplus(x):
    return jnp.maximum(x, 0.0) + jnp.log1p(jnp.exp(-jnp.abs(x)))


def _prep_kernel(main_ref, hb_ref, ha_ref, cw_ref, lg_ref, alog_ref, dtb_ref,
                 q_ref, k_ref, v_ref, gc_ref):
    i = pl.program_id(1)
    first = i == 0
    last = i == pl.num_programs(1) - 1
    tb = main_ref.shape[1]
    n_strips = 3 * A_W // LANES
    for s in range(n_strips):
        cs = slice(s * LANES, (s + 1) * LANES)
        xm = main_ref[0, :, cs].astype(F32)
        xb = jnp.where(first, 0.0, hb_ref[0, :, cs].astype(F32))
        xa = jnp.where(last, 0.0, ha_ref[0, :, cs].astype(F32))
        xc = jnp.concatenate([xb, xm, xa], axis=0)
        acc = None
        for j in range(CONV_W):
            shift = (CONV_W // 2 - j) % (tb + 2 * HALO)
            xs = xc if shift == 0 else pltpu.roll(xc, shift, axis=0)
            term = xs[HALO:HALO + tb] * cw_ref[j:j + 1, cs]
            acc = term if acc is None else acc + term
        y = acc * jax.nn.sigmoid(acc)
        head = s % A_HEADS
        hs = slice(head * LANES, (head + 1) * LANES)
        if s < A_HEADS:
            y = y * lax.rsqrt(jnp.sum(y * y, axis=-1, keepdims=True) + EPS) * (A_DK ** -0.5)
            q_ref[0, :, hs] = y.astype(BF16)
        elif s < 2 * A_HEADS:
            y = y * lax.rsqrt(jnp.sum(y * y, axis=-1, keepdims=True) + EPS)
            k_ref[0, :, hs] = y.astype(BF16)
        else:
            v_ref[0, :, hs] = y.astype(BF16)

    lg = lg_ref[0]
    g = -jnp.exp(alog_ref[...]) * _softplus(lg + dtb_ref[...])
    beta = jax.nn.sigmoid(lg)
    row = lax.broadcasted_iota(jnp.int32, (tb, LANES), 0) % A_CHUNK
    lane = lax.broadcasted_iota(jnp.int32, (tb, LANES), 1)
    pf = g
    sf = g
    step = 1
    while step < A_CHUNK:
        pf = pf + jnp.where(row >= step, pltpu.roll(pf, step, axis=0), 0.0)
        sf = sf + jnp.where(row < A_CHUNK - step, pltpu.roll(sf, tb - step, axis=0), 0.0)
        step *= 2
    gc_ref[0] = jnp.where(lane < A_HEADS, pf, jnp.where(lane < 2 * A_HEADS, sf, beta))


def _prep(p3, lg3, conv_w, alog_l, dtb_l, tb):
    b, l, _ = p3.shape
    nb = l // tb
    hpb = tb // HALO
    qkv_shape = jax.ShapeDtypeStruct((b, l, A_W), BF16)
    return pl.pallas_call(
        _prep_kernel,
        grid=(b, nb),
        in_specs=[
            pl.BlockSpec((1, tb, 3 * A_W), lambda bi, i: (bi, i, 0)),
            pl.BlockSpec((1, HALO, 3 * A_W), lambda bi, i: (bi, jnp.maximum(i * hpb - 1, 0), 0)),
            pl.BlockSpec((1, HALO, 3 * A_W),
                         lambda bi, i: (bi, jnp.minimum((i + 1) * hpb, l // HALO - 1), 0)),
            pl.BlockSpec((CONV_W, 3 * A_W), lambda bi, i: (0, 0)),
            pl.BlockSpec((1, tb, LANES), lambda bi, i: (bi, i, 0)),
            pl.BlockSpec((1, LANES), lambda bi, i: (0, 0)),
            pl.BlockSpec((1, LANES), lambda bi, i: (0, 0)),
        ],
        out_specs=[
            pl.BlockSpec((1, tb, A_W), lambda bi, i: (bi, i, 0)),
            pl.BlockSpec((1, tb, A_W), lambda bi, i: (bi, i, 0)),
            pl.BlockSpec((1, tb, A_W), lambda bi, i: (bi, i, 0)),
            pl.BlockSpec((1, tb, LANES), lambda bi, i: (bi, i, 0)),
        ],
        out_shape=[qkv_shape, qkv_shape, qkv_shape, jax.ShapeDtypeStruct((b, l, LANES), F32)],
        compiler_params=_params(("arbitrary", "arbitrary")),
    )(p3, p3, p3, conv_w, lg3, alog_l, dtb_l)


def _delta_chunk(q, k, v, gcol, bcol, grow, s_ref, backward):
    c = A_CHUNK
    ri = lax.broadcasted_iota(jnp.int32, (c, c), 0)
    ci = lax.broadcasted_iota(jnp.int32, (c, c), 1)
    if backward:
        incl, strict = ri <= ci, ri < ci
        glast = gcol[0:1, :]
    else:
        incl, strict = ri >= ci, ri > ci
        glast = gcol[c - 1:c, :]
    decay = jnp.where(incl, jnp.exp(gcol - grow), 0.0)
    eg = jnp.exp(gcol)
    kb = k * bcol
    k_bf = k.astype(BF16)
    nm = jnp.where(strict, -_dot_nt(kb.astype(BF16), k_bf) * decay, 0.0)
    rhs = jnp.concatenate([v * bcol, kb * eg], axis=1)
    step = 1
    while True:
        nm_bf = nm.astype(BF16)
        rhs = rhs + _dot(nm_bf, rhs.astype(BF16))
        step *= 2
        if step >= c:
            break
        nm = _dot(nm_bf, nm_bf)
    u = rhs[:, :A_DV]
    w = rhs[:, A_DV:]
    s_old = s_ref[...]
    s_bf = s_old.astype(BF16)
    v_new = u - _dot(w.astype(BF16), s_bf)
    v_new_bf = v_new.astype(BF16)
    attn = _dot_nt(q.astype(BF16), k_bf) * decay
    o = _dot((q * eg).astype(BF16), s_bf) + _dot(attn.astype(BF16), v_new_bf)
    kd = k * jnp.exp(glast - gcol)
    s_ref[...] = s_old * jnp.exp(glast) + _dot(kd.T.astype(BF16), v_new_bf)
    return o


def _scan_kernel(qf_ref, kf_ref, vf_ref, gf_ref, qb_ref, kb_ref, vb_ref, gb_ref,
                 of_ref, ob_ref, s_ref):
    @pl.when(pl.program_id(1) == 0)
    def _():
        s_ref[...] = jnp.zeros_like(s_ref)

    tb = qf_ref.shape[1]
    nc = tb // A_CHUNK

    def body(ci, carry):
        for backward, (q_ref, k_ref, v_ref, g_ref, o_ref) in enumerate(
                ((qf_ref, kf_ref, vf_ref, gf_ref, of_ref), (qb_ref, kb_ref, vb_ref, gb_ref, ob_ref))):
            cidx = nc - 1 - ci if backward else ci
            rows = pl.ds(pl.multiple_of(cidx * A_CHUNK, A_CHUNK), A_CHUNK)
            gtile = g_ref[0, rows, :]
            gt = gtile.T
            for h in range(A_HEADS):
                hs = slice(h * LANES, (h + 1) * LANES)
                gl = backward * A_HEADS + h
                bl = 2 * A_HEADS + gl
                o = _delta_chunk(
                    q_ref[0, rows, hs].astype(F32), k_ref[0, rows, hs].astype(F32),
                    v_ref[0, rows, hs].astype(F32),
                    gtile[:, gl:gl + 1], gtile[:, bl:bl + 1], gt[gl:gl + 1, :],
                    s_ref.at[gl], bool(backward))
                o_ref[0, rows, hs] = o.astype(BF16)
        return carry

    lax.fori_loop(0, nc, body, 0)


def _scan(q3, k3, v3, gc3, tb):
    b, l, _ = q3.shape
    nb = l // tb
    fwd = lambda bi, i: (bi, i, 0)
    bwd = lambda bi, i: (bi, nb - 1 - i, 0)
    big = lambda m: pl.BlockSpec((1, tb, A_W), m)
    small = lambda m: pl.BlockSpec((1, tb, LANES), m)
    o_shape = jax.ShapeDtypeStruct((b, l, A_HEADS * A_DV), BF16)
    return pl.pallas_call(
        _scan_kernel,
        grid=(b, nb),
        in_specs=[big(fwd), big(fwd), big(fwd), small(fwd), big(bwd), big(bwd), big(bwd), small(bwd)],
        out_specs=[big(fwd), big(bwd)],
        out_shape=[o_shape, o_shape],
        scratch_shapes=[pltpu.VMEM((2 * A_HEADS, A_DK, A_DV), F32)],
        compiler_params=_params(("arbitrary", "arbitrary")),
    )(q3, k3, v3, gc3, q3, k3, v3, gc3)


def _norm_rope(x, g, cos, sin_signed):
    y = x * lax.rsqrt(jnp.mean(x * x, axis=-1, keepdims=True) + EPS) * g
    lane = lax.broadcasted_iota(jnp.int32, y.shape, 1)
    quarter = B_HD // 4
    partner = jnp.where(lane % (2 * quarter) < quarter,
                        pltpu.roll(y, B_HD - quarter, axis=1), pltpu.roll(y, quarter, axis=1))
    return y * cos + partner * sin_signed


def _attn_kernel(q_ref, k_ref, v_ref, gate_ref, cosq_ref, sinq_ref, cosk_ref, sink_ref, qg_ref, kg_ref,
                 o_ref, kr_ref, qs_ref, m_ref, l_ref, acc_ref, *, tk):
    l = k_ref.shape[1]
    tq = q_ref.shape[1]

    @pl.when(pl.program_id(2) == 0)
    def _():
        def kbody(r, carry):
            sl = pl.ds(pl.multiple_of(r * tk, tk), tk)
            kr_ref[sl, :] = _norm_rope(k_ref[0, sl, :].astype(F32), kg_ref[...],
                                       cosk_ref[sl, :], sink_ref[sl, :]).astype(BF16)
            return carry
        lax.fori_loop(0, l // tk, kbody, 0)

    for hh in range(B_GROUP):
        qh = _norm_rope(q_ref[0, :, hh * B_HD:(hh + 1) * B_HD].astype(F32), qg_ref[...],
                        cosq_ref[...], sinq_ref[...])
        qs_ref[hh * tq:(hh + 1) * tq, :] = (qh * (B_HD ** -0.5)).astype(BF16)

    m_ref[...] = jnp.full_like(m_ref, -jnp.inf)
    l_ref[...] = jnp.zeros_like(l_ref)
    acc_ref[...] = jnp.zeros_like(acc_ref)

    def body(t, carry):
        sl = pl.ds(pl.multiple_of(t * tk, tk), tk)
        s = _dot_nt(qs_ref[...], kr_ref[sl, :])
        m_prev = m_ref[...]
        m_new = jnp.maximum(m_prev, jnp.max(s, axis=-1, keepdims=True))
        alpha = jnp.exp(m_prev - m_new)
        p = jnp.exp(s - m_new)
        l_ref[...] = alpha * l_ref[...] + jnp.sum(p, axis=-1, keepdims=True)
        acc_ref[...] = alpha * acc_ref[...] + _dot(p.astype(BF16), v_ref[0, sl, :])
        m_ref[...] = m_new
        return carry

    lax.fori_loop(0, l // tk, body, 0)

    o = acc_ref[...] / l_ref[...]
    for hh in range(B_GROUP):
        gate = gate_ref[0, :, hh * B_HD:(hh + 1) * B_HD].astype(F32)
        o_ref[0, :, hh * B_HD:(hh + 1) * B_HD] = (
            o[hh * tq:(hh + 1) * tq, :] * (gate * jax.nn.sigmoid(gate))).astype(BF16)


def _attn(p3, cos, sin_signed, qg, kg, tq, tk):
    b, l, _ = p3.shape
    gw = B_GROUP * B_HD
    return pl.pallas_call(
        functools.partial(_attn_kernel, tk=tk),
        grid=(b, B_KV_HEADS, l // tq),
        in_specs=[
            pl.BlockSpec((1, tq, gw), lambda bi, kv, i: (bi, i, COL_BQ // gw + kv)),
            pl.BlockSpec((1, l, B_HD), lambda bi, kv, i: (bi, 0, COL_BK // B_HD + kv)),
            pl.BlockSpec((1, l, B_HD), lambda bi, kv, i: (bi, 0, COL_BV // B_HD + kv)),
            pl.BlockSpec((1, tq, gw), lambda bi, kv, i: (bi, i, COL_BGATE // gw + kv)),
            pl.BlockSpec((tq, B_HD), lambda bi, kv, i: (i, 0)),
            pl.BlockSpec((tq, B_HD), lambda bi, kv, i: (i, 0)),
            pl.BlockSpec((l, B_HD), lambda bi, kv, i: (0, 0)),
            pl.BlockSpec((l, B_HD), lambda bi, kv, i: (0, 0)),
            pl.BlockSpec((1, B_HD), lambda bi, kv, i: (0, 0)),
            pl.BlockSpec((1, B_HD), lambda bi, kv, i: (0, 0)),
        ],
        out_specs=pl.BlockSpec((1, tq, gw), lambda bi, kv, i: (bi, i, kv)),
        out_shape=jax.ShapeDtypeStruct((b, l, B_W), BF16),
        scratch_shapes=[
            pltpu.VMEM((l, B_HD), BF16),
            pltpu.VMEM((B_GROUP * tq, B_HD), BF16),
            pltpu.VMEM((B_GROUP * tq, 1), F32),
            pltpu.VMEM((B_GROUP * tq, 1), F32),
            pltpu.VMEM((B_GROUP * tq, B_HD), F32),
        ],
        compiler_params=_params(("arbitrary", "arbitrary", "arbitrary")),
    )(p3, p3, p3, p3, cos, sin_signed, cos, sin_signed, qg, kg)


def _out0_kernel(of_ref, ob_ref, ag_ref, yb_ref, og_ref, w_ref, x_ref, o_ref, y_ref):
    @pl.when(pl.program_id(1) == 0)
    def _():
        for h in range(A_HEADS):
            hs = slice(h * A_DV, (h + 1) * A_DV)
            o = of_ref[:, hs].astype(F32) + ob_ref[:, hs].astype(F32)
            o = o * lax.rsqrt(jnp.mean(o * o, axis=-1, keepdims=True) + EPS) * og_ref[...]
            gate = ag_ref[:, hs].astype(F32)
            y_ref[:, hs] = (o * (gate * jax.nn.sigmoid(gate))).astype(BF16)
        y_ref[:, A_HEADS * A_DV:] = yb_ref[...]

    o_ref[...] = x_ref[...] + _dot(y_ref[...], w_ref[...])


def _out0(o_f, o_b, p2, yb, og, w_out, x2, tm, tn):
    t, d = x2.shape
    aw = A_HEADS * A_DV
    kdim = aw + B_W
    return pl.pallas_call(
        _out0_kernel,
        grid=(t // tm, d // tn),
        in_specs=[
            pl.BlockSpec((tm, aw), lambda i, j: (i, 0)),
            pl.BlockSpec((tm, aw), lambda i, j: (i, 0)),
            pl.BlockSpec((tm, aw), lambda i, j: (i, COL_AGATE // aw)),
            pl.BlockSpec((tm, B_W), lambda i, j: (i, 0)),
            pl.BlockSpec((1, A_DV), lambda i, j: (0, 0)),
            pl.BlockSpec((kdim, tn), lambda i, j: (0, j)),
            pl.BlockSpec((tm, tn), lambda i, j: (i, j)),
        ],
        out_specs=pl.BlockSpec((tm, tn), lambda i, j: (i, j)),
        out_shape=jax.ShapeDtypeStruct((t, d), F32),
        scratch_shapes=[pltpu.VMEM((tm, kdim), BF16)],
        compiler_params=_params(("arbitrary", "arbitrary")),
    )(o_f, o_b, p2, yb, og, w_out, x2)


def _gelu(x):
    return 0.5 * x * (1.0 + lax.erf(x * (2.0 ** -0.5)))


def _proj1_kernel(x_ref, g_ref, w_ref, p_ref, h_ref, *, gate_start):
    @pl.when(pl.program_id(1) == 0)
    def _():
        _rmsnorm_rows(x_ref, g_ref, h_ref, 128)

    y = _dot(h_ref[...], w_ref[...])

    @pl.when(pl.program_id(1) < gate_start)
    def _():
        p_ref[...] = _gelu(y).astype(BF16)

    @pl.when(pl.program_id(1) >= gate_start)
    def _():
        p_ref[...] = (y * jax.nn.sigmoid(y)).astype(BF16)


def _proj1(x2, g, w, tm, tn):
    t, d = x2.shape
    n = w.shape[1]
    return pl.pallas_call(
        functools.partial(_proj1_kernel, gate_start=(2 * n // 3) // tn),
        grid=(t // tm, n // tn),
        in_specs=[
            pl.BlockSpec((tm, d), lambda i, j: (i, 0)),
            pl.BlockSpec((1, d), lambda i, j: (0, 0)),
            pl.BlockSpec((d, tn), lambda i, j: (0, j)),
        ],
        out_specs=pl.BlockSpec((tm, tn), lambda i, j: (i, j)),
        out_shape=jax.ShapeDtypeStruct((t, n), BF16),
        scratch_shapes=[pltpu.VMEM((tm, d), BF16)],
        compiler_params=_params(("arbitrary", "arbitrary")),
    )(x2, g, w)


def _out1_kernel(u_ref, v_ref, gt_ref, lng_ref, lnb_ref, ws_ref, bst_ref, w_ref, x_ref, o_ref, z_ref):
    tm = u_ref.shape[0]
    cw = v_ref.shape[1]
    gw = cw // C_GROUPS

    @pl.when(pl.program_id(1) == 0)
    def _():
        def chunk(c, carry):
            rows = pl.ds(pl.multiple_of(c * C_CHUNK, C_CHUNK), C_CHUNK)
            v = v_ref[rows, :].astype(F32)
            mu = jnp.mean(v, axis=-1, keepdims=True)
            vc = v - mu
            var = jnp.mean(vc * vc, axis=-1, keepdims=True)
            vn = (vc * lax.rsqrt(var + EPS) * lng_ref[...] + lnb_ref[...]).astype(BF16)
            for g in range(C_GROUPS):
                gs = slice(g * gw, (g + 1) * gw)
                s = _dot(ws_ref[g], vn[:, gs]) + bst_ref[:, g:g + 1]
                z = u_ref[rows, gs].astype(F32) * s * gt_ref[rows, gs].astype(F32)
                z_ref[rows, gs] = z.astype(BF16)
            return carry
        lax.fori_loop(0, tm // C_CHUNK, chunk, 0)

    o_ref[...] = x_ref[...] + _dot(z_ref[...], w_ref[...])


def _out1(p2, lng, lnb, ws, bst, w_out, x2, tm, tn):
    t, d = x2.shape
    cw = w_out.shape[0]
    return pl.pallas_call(
        _out1_kernel,
        grid=(t // tm, d // tn),
        in_specs=[
            pl.BlockSpec((tm, cw), lambda i, j: (i, 0)),
            pl.BlockSpec((tm, cw), lambda i, j: (i, 1)),
            pl.BlockSpec((tm, cw), lambda i, j: (i, 2)),
            pl.BlockSpec((1, cw), lambda i, j: (0, 0)),
            pl.BlockSpec((1, cw), lambda i, j: (0, 0)),
            pl.BlockSpec((C_GROUPS, C_CHUNK, C_CHUNK), lambda i, j: (0, 0, 0)),
            pl.BlockSpec((C_CHUNK, C_GROUPS), lambda i, j: (0, 0)),
            pl.BlockSpec((cw, tn), lambda i, j: (0, j)),
            pl.BlockSpec((tm, tn), lambda i, j: (i, j)),
        ],
        out_specs=pl.BlockSpec((tm, tn), lambda i, j: (i, j)),
        out_shape=jax.ShapeDtypeStruct((t, d), F32),
        scratch_shapes=[pltpu.VMEM((tm, cw), BF16)],
        compiler_params=_params(("arbitrary", "arbitrary")),
    )(p2, p2, p2, lng, lnb, ws, bst, w_out, x2)


def _rope_tables(l):
    t = jnp.arange(l)
    row = (t // GRID_W).astype(F32)
    col = (t % GRID_W).astype(F32)
    n_freq = B_HD // 4
    inv = ROPE_THETA ** (-jnp.arange(n_freq, dtype=F32) / n_freq)
    ar = row[:, None] * inv
    ac = col[:, None] * inv
    cos = jnp.concatenate([jnp.cos(ar), jnp.cos(ar), jnp.cos(ac), jnp.cos(ac)], axis=-1)
    sin = jnp.concatenate([-jnp.sin(ar), jnp.sin(ar), -jnp.sin(ac), jnp.sin(ac)], axis=-1)
    return cos, sin


def _lane_row(v):
    v = v.reshape(1, -1).astype(F32)
    return jnp.pad(v, ((0, 0), (0, LANES - v.shape[1])))


def _pick(n, pref):
    while n % pref:
        pref //= 2
    return pref


def _hybrid_layer(x2, b, l, norm_g, w_in, conv_w, a_log, dt_bias, onorm_g, qn_g, kn_g, w_out):
    t, d = x2.shape
    sizes = (A_W, A_W, A_HEADS * A_DV, A_HEADS * A_DV, 4 * A_HEADS, B_W, B_KV_W, B_KV_W, B_W)
    offs = [0]
    for s in sizes:
        offs.append(offs[-1] + s)
    aq, ak, av, agate, alog_in, bq, bk, bv, bgate = [w_in[:, offs[i]:offs[i + 1]] for i in range(9)]
    w_main = jnp.concatenate([aq, ak, av, agate, bq, bgate, bk, bv], axis=1).astype(BF16)
    w_log = jnp.pad(alog_in, ((0, 0), (0, LANES - alog_in.shape[1]))).astype(BF16)

    tm = _pick(t, 1024)
    p2, lg2 = _proj0(x2, norm_g.reshape(1, d), w_main, w_log, tm, 512)
    p3 = p2.reshape(b, l, P0_COLS)
    lg3 = lg2.reshape(b, l, LANES)

    tb = _pick(l, 512)
    q3, k3, v3, gc3 = _prep(p3, lg3, conv_w.astype(F32), _lane_row(a_log), _lane_row(dt_bias), tb)
    o_f, o_b = _scan(q3, k3, v3, gc3, tb)

    cos, sin_signed = _rope_tables(l)
    yb = _attn(p3, cos, sin_signed, qn_g.reshape(1, B_HD).astype(F32), kn_g.reshape(1, B_HD).astype(F32),
               _pick(l, 256), _pick(l, 512))

    return _out0(o_f.reshape(t, -1), o_b.reshape(t, -1), p2, yb.reshape(t, B_W),
                 onorm_g.reshape(1, A_DV).astype(F32), w_out.astype(BF16), x2, _pick(t, 512), 1024)


def _gmlp_layer(x2, norm_g, w_in, ln_g, ln_b, w_s, b_s, w_out):
    t, d = x2.shape
    cw = w_out.shape[0]
    p2 = _proj1(x2, norm_g.reshape(1, d), w_in.astype(BF16), _pick(t, 1024), 512)
    return _out1(p2, ln_g.reshape(1, cw).astype(F32), ln_b.reshape(1, cw).astype(F32),
                 w_s.astype(BF16), jnp.transpose(b_s).astype(F32), w_out.astype(BF16), x2,
                 _pick(t, 512), 1024)


def kernel(x, norm0_g, w_in0, conv0_w, a_log0, dt_bias0, a_onorm_g0, b_qnorm_g0, b_knorm_g0, w_out0,
           norm1_g, w_in1, c_ln_g1, c_ln_b1, c_ws1, c_bs1, w_out1):
    b, l, d = x.shape
    depth = w_in0.shape[0] + w_in1.shape[0]
    x2 = x.reshape(b * l, d)
    for layer in range(depth):
        i = layer // 2
        if layer % 2 == 0:
            x2 = _hybrid_layer(x2, b, l, norm0_g[i], w_in0[i], conv0_w[i], a_log0[i], dt_bias0[i],
                               a_onorm_g0[i], b_qnorm_g0[i], b_knorm_g0[i], w_out0[i])
        else:
            x2 = _gmlp_layer(x2, norm1_g[i], w_in1[i], c_ln_g1[i], c_ln_b1[i], c_ws1[i], c_bs1[i], w_out1[i])
    return x2.reshape(b, l, d)
```

```python
import functools
import math

import jax
import jax.numpy as jnp
from jax import lax
from jax.experimental import pallas as pl
from jax.experimental.pallas import tpu as pltpu

F32 = jnp.float32
BF16 = jnp.bfloat16

EPS = 1e-6
GRID_W = 64
ROPE_THETA = 10000.0
A_HEADS = 8
A_DK = 128
A_DV = 128
A_CHUNK = 64
CONV_W = 5
A_W = A_HEADS * A_DK
B_HEADS = 8
B_KV_HEADS = 2
B_HD = 128
B_GROUP = B_HEADS // B_KV_HEADS
B_W = B_HEADS * B_HD
B_KV_W = B_KV_HEADS * B_HD
C_GROUPS = 16
C_CHUNK = 128

LANES = 128
VMEM_LIMIT = 56 * 1024 * 1024

COL_AQ = 0
COL_AK = A_W
COL_AV = 2 * A_W
COL_AGATE = 3 * A_W
COL_BQ = 4 * A_W
COL_BGATE = COL_BQ + B_W
COL_BK = COL_BGATE + B_W
COL_BV = COL_BK + B_KV_W
P0_COLS = COL_BV + B_KV_W


def _params(sem):
    return pltpu.CompilerParams(dimension_semantics=sem, vmem_limit_bytes=VMEM_LIMIT)


def _dot(a, b):
    return jnp.dot(a, b, preferred_element_type=F32)


def _dot_nt(a, b):
    return lax.dot_general(a, b, (((1,), (1,)), ((), ())), preferred_element_type=F32)


def _rmsnorm_rows(x_ref, g_ref, h_ref, rows):
    def body(r, carry):
        sl = pl.ds(pl.multiple_of(r * rows, rows), rows)
        x = x_ref[sl, :]
        ms = jnp.mean(x * x, axis=-1, keepdims=True)
        h_ref[sl, :] = (x * lax.rsqrt(ms + EPS) * g_ref[...]).astype(BF16)
        return carry
    lax.fori_loop(0, x_ref.shape[0] // rows, body, 0)


def _proj0_kernel(x_ref, g_ref, w_ref, wl_ref, p_ref, lg_ref, h_ref):
    @pl.when(pl.program_id(1) == 0)
    def _():
        _rmsnorm_rows(x_ref, g_ref, h_ref, 128)
        lg_ref[...] = _dot(h_ref[...], wl_ref[...])

    p_ref[...] = _dot(h_ref[...], w_ref[...]).astype(BF16)


def _proj0(x2, g, w_main, w_log, tm, tn):
    t, d = x2.shape
    n = w_main.shape[1]
    return pl.pallas_call(
        _proj0_kernel,
        grid=(t // tm, n // tn),
        in_specs=[
            pl.BlockSpec((tm, d), lambda i, j: (i, 0)),
            pl.BlockSpec((1, d), lambda i, j: (0, 0)),
            pl.BlockSpec((d, tn), lambda i, j: (0, j)),
            pl.BlockSpec((d, LANES), lambda i, j: (0, 0)),
        ],
        out_specs=[
            pl.BlockSpec((tm, tn), lambda i, j: (i, j)),
            pl.BlockSpec((tm, LANES), lambda i, j: (i, 0)),
        ],
        out_shape=[
            jax.ShapeDtypeStruct((t, n), BF16),
            jax.ShapeDtypeStruct((t, LANES), F32),
        ],
        scratch_shapes=[pltpu.VMEM((tm, d), BF16)],
        compiler_params=_params(("arbitrary", "arbitrary")),
    )(x2, g, w_main, w_log)


HALO = 16


def _softplus(x):
    return jnp.maximum(x, 0.0) + jnp.log1p(jnp.exp(-jnp.abs(x)))


def _prep_kernel(main_ref, hb_ref, ha_ref, cw_ref, lg_ref, alog_ref, dtb_ref,
                 q_ref, k_ref, v_ref, gc_ref):
    i = pl.program_id(1)
    first = i == 0
    last = i == pl.num_programs(1) - 1
    tb = main_ref.shape[1]
    n_strips = 3 * A_W // LANES
    for s in range(n_strips):
        cs = slice(s * LANES, (s + 1) * LANES)
        xm = main_ref[0, :, cs].astype(F32)
        xb = jnp.where(first, 0.0, hb_ref[0, :, cs].astype(F32))
        xa = jnp.where(last, 0.0, ha_ref[0, :, cs].astype(F32))
        xc = jnp.concatenate([xb, xm, xa], axis=0)
        acc = None
        for j in range(CONV_W):
            shift = (CONV_W // 2 - j) % (tb + 2 * HALO)
            xs = xc if shift == 0 else pltpu.roll(xc, shift, axis=0)
            term = xs[HALO:HALO + tb] * cw_ref[j:j + 1, cs]
            acc = term if acc is None else acc + term
        y = acc * jax.nn.sigmoid(acc)
        head = s % A_HEADS
        hs = slice(head * LANES, (head + 1) * LANES)
        if s < A_HEADS:
            y = y * lax.rsqrt(jnp.sum(y * y, axis=-1, keepdims=True) + EPS) * (A_DK ** -0.5)
            q_ref[0, :, hs] = y.astype(BF16)
        elif s < 2 * A_HEADS:
            y = y * lax.rsqrt(jnp.sum(y * y, axis=-1, keepdims=True) + EPS)
            k_ref[0, :, hs] = y.astype(BF16)
        else:
            v_ref[0, :, hs] = y.astype(BF16)

    lg = lg_ref[0]
    g = -jnp.exp(alog_ref[...]) * _softplus(lg + dtb_ref[...])
    beta = jax.nn.sigmoid(lg)
    row = lax.broadcasted_iota(jnp.int32, (tb, LANES), 0) % A_CHUNK
    lane = lax.broadcasted_iota(jnp.int32, (tb, LANES), 1)
    pf = g
    sf = g
    step = 1
    while step < A_CHUNK:
        pf = pf + jnp.where(row >= step, pltpu.roll(pf, step, axis=0), 0.0)
        sf = sf + jnp.where(row < A_CHUNK - step, pltpu.roll(sf, tb - step, axis=0), 0.0)
        step *= 2
    gc_ref[0] = jnp.where(lane < A_HEADS, pf, jnp.where(lane < 2 * A_HEADS, sf, beta))


def _prep(p3, lg3, conv_w, alog_l, dtb_l, tb):
    b, l, _ = p3.shape
    nb = l // tb
    hpb = tb // HALO
    qkv_shape = jax.ShapeDtypeStruct((b, l, A_W), BF16)
    return pl.pallas_call(
        _prep_kernel,
        grid=(b, nb),
        in_specs=[
            pl.BlockSpec((1, tb, 3 * A_W), lambda bi, i: (bi, i, 0)),
            pl.BlockSpec((1, HALO, 3 * A_W), lambda bi, i: (bi, jnp.maximum(i * hpb - 1, 0), 0)),
            pl.BlockSpec((1, HALO, 3 * A_W),
                         lambda bi, i: (bi, jnp.minimum((i + 1) * hpb, l // HALO - 1), 0)),
            pl.BlockSpec((CONV_W, 3 * A_W), lambda bi, i: (0, 0)),
            pl.BlockSpec((1, tb, LANES), lambda bi, i: (bi, i, 0)),
            pl.BlockSpec((1, LANES), lambda bi, i: (0, 0)),
            pl.BlockSpec((1, LANES), lambda bi, i: (0, 0)),
        ],
        out_specs=[
            pl.BlockSpec((1, tb, A_W), lambda bi, i: (bi, i, 0)),
            pl.BlockSpec((1, tb, A_W), lambda bi, i: (bi, i, 0)),
            pl.BlockSpec((1, tb, A_W), lambda bi, i: (bi, i, 0)),
            pl.BlockSpec((1, tb, LANES), lambda bi, i: (bi, i, 0)),
        ],
        out_shape=[qkv_shape, qkv_shape, qkv_shape, jax.ShapeDtypeStruct((b, l, LANES), F32)],
        compiler_params=_params(("arbitrary", "arbitrary")),
    )(p3, p3, p3, conv_w, lg3, alog_l, dtb_l)


def _delta_chunk(q, k, v, gcol, bcol, grow, s_ref, backward):
    c = A_CHUNK
    ri = lax.broadcasted_iota(jnp.int32, (c, c), 0)
    ci = lax.broadcasted_iota(jnp.int32, (c, c), 1)
    if backward:
        incl, strict = ri <= ci, ri < ci
        glast = gcol[0:1, :]
    else:
        incl, strict = ri >= ci, ri > ci
        glast = gcol[c - 1:c, :]
    decay = jnp.where(incl, jnp.exp(gcol - grow), 0.0)
    eg = jnp.exp(gcol)
    kb = k * bcol
    k_bf = k.astype(BF16)
    nm = jnp.where(strict, -_dot_nt(kb.astype(BF16), k_bf) * decay, 0.0)
    rhs = jnp.concatenate([v * bcol, kb * eg], axis=1)
    step = 1
    while True:
        nm_bf = nm.astype(BF16)
        rhs = rhs + _dot(nm_bf, rhs.astype(BF16))
        step *= 2
        if step >= c:
            break
        nm = _dot(nm_bf, nm_bf)
    u = rhs[:, :A_DV]
    w = rhs[:, A_DV:]
    s_old = s_ref[...]
    s_bf = s_old.astype(BF16)
    v_new = u - _dot(w.astype(BF16), s_bf)
    v_new_bf = v_new.astype(BF16)
    attn = _dot_nt(q.astype(BF16), k_bf) * decay
    o = _dot((q * eg).astype(BF16), s_bf) + _dot(attn.astype(BF16), v_new_bf)
    kd = k * jnp.exp(glast - gcol)
    s_ref[...] = s_old * jnp.exp(glast) + _dot(kd.T.astype(BF16), v_new_bf)
    return o


def _scan_kernel(qf_ref, kf_ref, vf_ref, gf_ref, qb_ref, kb_ref, vb_ref, gb_ref,
                 of_ref, ob_ref, s_ref):
    @pl.when(pl.program_id(1) == 0)
    def _():
        s_ref[...] = jnp.zeros_like(s_ref)

    tb = qf_ref.shape[1]
    nc = tb // A_CHUNK

    def body(ci, carry):
        for backward, (q_ref, k_ref, v_ref, g_ref, o_ref) in enumerate(
                ((qf_ref, kf_ref, vf_ref, gf_ref, of_ref), (qb_ref, kb_ref, vb_ref, gb_ref, ob_ref))):
            cidx = nc - 1 - ci if backward else ci
            rows = pl.ds(pl.multiple_of(cidx * A_CHUNK, A_CHUNK), A_CHUNK)
            gtile = g_ref[0, rows, :]
            gt = gtile.T
            for h in range(A_HEADS):
                hs = slice(h * LANES, (h + 1) * LANES)
                gl = backward * A_HEADS + h
                bl = 2 * A_HEADS + gl
                o = _delta_chunk(
                    q_ref[0, rows, hs].astype(F32), k_ref[0, rows, hs].astype(F32),
                    v_ref[0, rows, hs].astype(F32),
                    gtile[:, gl:gl + 1], gtile[:, bl:bl + 1], gt[gl:gl + 1, :],
                    s_ref.at[gl], bool(backward))
                o_ref[0, rows, hs] = o.astype(BF16)
        return carry

    lax.fori_loop(0, nc, body, 0)


def _scan(q3, k3, v3, gc3, tb):
    b, l, _ = q3.shape
    nb = l // tb
    fwd = lambda bi, i: (bi, i, 0)
    bwd = lambda bi, i: (bi, nb - 1 - i, 0)
    big = lambda m: pl.BlockSpec((1, tb, A_W), m)
    small = lambda m: pl.BlockSpec((1, tb, LANES), m)
    o_shape = jax.ShapeDtypeStruct((b, l, A_HEADS * A_DV), BF16)
    return pl.pallas_call(
        _scan_kernel,
        grid=(b, nb),
        in_specs=[big(fwd), big(fwd), big(fwd), small(fwd), big(bwd), big(bwd), big(bwd), small(bwd)],
        out_specs=[big(fwd), big(bwd)],
        out_shape=[o_shape, o_shape],
        scratch_shapes=[pltpu.VMEM((2 * A_HEADS, A_DK, A_DV), F32)],
        compiler_params=_params(("arbitrary", "arbitrary")),
    )(q3, k3, v3, gc3, q3, k3, v3, gc3)


def _norm_rope(x, g, cos, sin_signed):
    y = x * lax.rsqrt(jnp.mean(x * x, axis=-1, keepdims=True) + EPS) * g
    lane = lax.broadcasted_iota(jnp.int32, y.shape, 1)
    quarter = B_HD // 4
    partner = jnp.where(lane % (2 * quarter) < quarter,
                        pltpu.roll(y, B_HD - quarter, axis=1), pltpu.roll(y, quarter, axis=1))
    return y * cos + partner * sin_signed


def _attn_kernel(q_ref, k_ref, v_ref, gate_ref, cosq_ref, sinq_ref, cosk_ref, sink_ref, qg_ref, kg_ref,
                 o_ref, kr_ref, vx_ref, qs_ref, s_ref, m_ref, acc_ref, *, tk):
    l = k_ref.shape[1]
    tq = q_ref.shape[1]

    @pl.when(pl.program_id(2) == 0)
    def _():
        def kbody(r, carry):
            sl = pl.ds(pl.multiple_of(r * tk, tk), tk)
            kr_ref[sl, :] = _norm_rope(k_ref[0, sl, :].astype(F32), kg_ref[...],
                                       cosk_ref[sl, :], sink_ref[sl, :]).astype(BF16)
            vx_ref[sl, :B_HD] = v_ref[0, sl, :]
            vx_ref[sl, B_HD:] = jnp.ones((tk, B_HD), BF16)
            return carry
        lax.fori_loop(0, l // tk, kbody, 0)

    qscale = (B_HD ** -0.5) * math.log2(math.e)
    for hh in range(B_GROUP):
        qh = _norm_rope(q_ref[0, :, hh * B_HD:(hh + 1) * B_HD].astype(F32), qg_ref[...],
                        cosq_ref[...], sinq_ref[...])
        qs_ref[hh * tq:(hh + 1) * tq, :] = (qh * qscale).astype(BF16)

    m_ref[...] = jnp.full_like(m_ref, -jnp.inf)
    acc_ref[...] = jnp.zeros_like(acc_ref)

    def kv_rows(t):
        if isinstance(t, int):
            return slice(t * tk, (t + 1) * tk)
        return pl.ds(pl.multiple_of(t * tk, tk), tk)

    def scores(t, buf):
        sl = kv_rows(t)
        k_blk = kr_ref[sl, :]
        for hh in range(B_GROUP):
            rows = slice(hh * tq, (hh + 1) * tq)
            s_ref[buf, rows, :] = _dot_nt(qs_ref[rows, :], k_blk)

    def softmax_pv(t, buf):
        sl = kv_rows(t)
        v_blk = vx_ref[sl, :]
        for hh in range(B_GROUP):
            rows = slice(hh * tq, (hh + 1) * tq)
            s = s_ref[buf, rows, :]
            m_prev = m_ref[rows, :]
            m_new = jnp.maximum(m_prev, jnp.max(s, axis=-1, keepdims=True))
            alpha = jnp.exp2(m_prev - m_new)
            p = jnp.concatenate(
                [jnp.exp2(s[:, c * LANES:(c + 1) * LANES] - m_new) for c in range(tk // LANES)],
                axis=1).astype(BF16)
            acc_ref[rows, :] = jnp.concatenate([alpha, alpha], axis=1) * acc_ref[rows, :] + _dot(p, v_blk)
            m_ref[rows, :] = m_new

    n_pairs = l // (2 * tk)
    scores(0, 0)

    def body(i, carry):
        scores(2 * i + 1, 1)
        softmax_pv(2 * i, 0)
        scores(2 * i + 2, 0)
        softmax_pv(2 * i + 1, 1)
        return carry

    lax.fori_loop(0, n_pairs - 1, body, 0)
    scores(2 * n_pairs - 1, 1)
    softmax_pv(2 * n_pairs - 2, 0)
    softmax_pv(2 * n_pairs - 1, 1)

    for hh in range(B_GROUP):
        rows = slice(hh * tq, (hh + 1) * tq)
        o = acc_ref[rows, :B_HD] / acc_ref[rows, B_HD:]
        gate = gate_ref[0, :, hh * B_HD:(hh + 1) * B_HD].astype(F32)
        o_ref[0, :, hh * B_HD:(hh + 1) * B_HD] = (o * (gate * jax.nn.sigmoid(gate))).astype(BF16)


def _attn(p3, cos, sin_signed, qg, kg, tq, tk):
    b, l, _ = p3.shape
    gw = B_GROUP * B_HD
    return pl.pallas_call(
        functools.partial(_attn_kernel, tk=tk),
        grid=(b, B_KV_HEADS, l // tq),
        in_specs=[
            pl.BlockSpec((1, tq, gw), lambda bi, kv, i: (bi, i, COL_BQ // gw + kv)),
            pl.BlockSpec((1, l, B_HD), lambda bi, kv, i: (bi, 0, COL_BK // B_HD + kv)),
            pl.BlockSpec((1, l, B_HD), lambda bi, kv, i: (bi, 0, COL_BV // B_HD + kv)),
            pl.BlockSpec((1, tq, gw), lambda bi, kv, i: (bi, i, COL_BGATE // gw + kv)),
            pl.BlockSpec((tq, B_HD), lambda bi, kv, i: (i, 0)),
            pl.BlockSpec((tq, B_HD), lambda bi, kv, i: (i, 0)),
            pl.BlockSpec((l, B_HD), lambda bi, kv, i: (0, 0)),
            pl.BlockSpec((l, B_HD), lambda bi, kv, i: (0, 0)),
            pl.BlockSpec((1, B_HD), lambda bi, kv, i: (0, 0)),
            pl.BlockSpec((1, B_HD), lambda bi, kv, i: (0, 0)),
        ],
        out_specs=pl.BlockSpec((1, tq, gw), lambda bi, kv, i: (bi, i, kv)),
        out_shape=jax.ShapeDtypeStruct((b, l, B_W), BF16),
        scratch_shapes=[
            pltpu.VMEM((l, B_HD), BF16),
            pltpu.VMEM((l, 2 * B_HD), BF16),
            pltpu.VMEM((B_GROUP * tq, B_HD), BF16),
            pltpu.VMEM((2, B_GROUP * tq, tk), F32),
            pltpu.VMEM((B_GROUP * tq, B_HD), F32),
            pltpu.VMEM((B_GROUP * tq, 2 * B_HD), F32),
        ],
        compiler_params=_params(("arbitrary", "arbitrary", "arbitrary")),
    )(p3, p3, p3, p3, cos, sin_signed, cos, sin_signed, qg, kg)


def _out0_kernel(of_ref, ob_ref, ag_ref, yb_ref, og_ref, w_ref, x_ref, o_ref, y_ref):
    @pl.when(pl.program_id(1) == 0)
    def _():
        for h in range(A_HEADS):
            hs = slice(h * A_DV, (h + 1) * A_DV)
            o = of_ref[:, hs].astype(F32) + ob_ref[:, hs].astype(F32)
            o = o * lax.rsqrt(jnp.mean(o * o, axis=-1, keepdims=True) + EPS) * og_ref[...]
            gate = ag_ref[:, hs].astype(F32)
            y_ref[:, hs] = (o * (gate * jax.nn.sigmoid(gate))).astype(BF16)
        y_ref[:, A_HEADS * A_DV:] = yb_ref[...]

    o_ref[...] = x_ref[...] + _dot(y_ref[...], w_ref[...])


def _out0(o_f, o_b, p2, yb, og, w_out, x2, tm, tn):
    t, d = x2.shape
    aw = A_HEADS * A_DV
    kdim = aw + B_W
    return pl.pallas_call(
        _out0_kernel,
        grid=(t // tm, d // tn),
        in_specs=[
            pl.BlockSpec((tm, aw), lambda i, j: (i, 0)),
            pl.BlockSpec((tm, aw), lambda i, j: (i, 0)),
            pl.BlockSpec((tm, aw), lambda i, j: (i, COL_AGATE // aw)),
            pl.BlockSpec((tm, B_W), lambda i, j: (i, 0)),
            pl.BlockSpec((1, A_DV), lambda i, j: (0, 0)),
            pl.BlockSpec((kdim, tn), lambda i, j: (0, j)),
            pl.BlockSpec((tm, tn), lambda i, j: (i, j)),
        ],
        out_specs=pl.BlockSpec((tm, tn), lambda i, j: (i, j)),
        out_shape=jax.ShapeDtypeStruct((t, d), F32),
        scratch_shapes=[pltpu.VMEM((tm, kdim), BF16)],
        compiler_params=_params(("arbitrary", "arbitrary")),
    )(o_f, o_b, p2, yb, og, w_out, x2)


def _gelu(x):
    return 0.5 * x * (1.0 + lax.erf(x * (2.0 ** -0.5)))


def _proj1_kernel(x_ref, g_ref, w_ref, p_ref, h_ref, *, gate_start):
    @pl.when(pl.program_id(1) == 0)
    def _():
        _rmsnorm_rows(x_ref, g_ref, h_ref, 128)

    y = _dot(h_ref[...], w_ref[...])

    @pl.when(pl.program_id(1) < gate_start)
    def _():
        p_ref[...] = _gelu(y).astype(BF16)

    @pl.when(pl.program_id(1) >= gate_start)
    def _():
        p_ref[...] = (y * jax.nn.sigmoid(y)).astype(BF16)


def _proj1(x2, g, w, tm, tn):
    t, d = x2.shape
    n = w.shape[1]
    return pl.pallas_call(
        functools.partial(_proj1_kernel, gate_start=(2 * n // 3) // tn),
        grid=(t // tm, n // tn),
        in_specs=[
            pl.BlockSpec((tm, d), lambda i, j: (i, 0)),
            pl.BlockSpec((1, d), lambda i, j: (0, 0)),
            pl.BlockSpec((d, tn), lambda i, j: (0, j)),
        ],
        out_specs=pl.BlockSpec((tm, tn), lambda i, j: (i, j)),
        out_shape=jax.ShapeDtypeStruct((t, n), BF16),
        scratch_shapes=[pltpu.VMEM((tm, d), BF16)],
        compiler_params=_params(("arbitrary", "arbitrary")),
    )(x2, g, w)


def _out1_kernel(u_ref, v_ref, gt_ref, lng_ref, lnb_ref, ws_ref, bst_ref, w_ref, x_ref, o_ref, z_ref):
    tm = u_ref.shape[0]
    cw = v_ref.shape[1]
    gw = cw // C_GROUPS

    @pl.when(pl.program_id(1) == 0)
    def _():
        def chunk(c, carry):
            rows = pl.ds(pl.multiple_of(c * C_CHUNK, C_CHUNK), C_CHUNK)
            v = v_ref[rows, :].astype(F32)
            mu = jnp.mean(v, axis=-1, keepdims=True)
            vc = v - mu
            var = jnp.mean(vc * vc, axis=-1, keepdims=True)
            vn = (vc * lax.rsqrt(var + EPS) * lng_ref[...] + lnb_ref[...]).astype(BF16)
            for g in range(C_GROUPS):
                gs = slice(g * gw, (g + 1) * gw)
                s = _dot(ws_ref[g], vn[:, gs]) + bst_ref[:, g:g + 1]
                z = u_ref[rows, gs].astype(F32) * s * gt_ref[rows, gs].astype(F32)
                z_ref[rows, gs] = z.astype(BF16)
            return carry
        lax.fori_loop(0, tm // C_CHUNK, chunk, 0)

    o_ref[...] = x_ref[...] + _dot(z_ref[...], w_ref[...])


def _out1(p2, lng, lnb, ws, bst, w_out, x2, tm, tn):
    t, d = x2.shape
    cw = w_out.shape[0]
    return pl.pallas_call(
        _out1_kernel,
        grid=(t // tm, d // tn),
        in_specs=[
            pl.BlockSpec((tm, cw), lambda i, j: (i, 0)),
            pl.BlockSpec((tm, cw), lambda i, j: (i, 1)),
            pl.BlockSpec((tm, cw), lambda i, j: (i, 2)),
            pl.BlockSpec((1, cw), lambda i, j: (0, 0)),
            pl.BlockSpec((1, cw), lambda i, j: (0, 0)),
            pl.BlockSpec((C_GROUPS, C_CHUNK, C_CHUNK), lambda i, j: (0, 0, 0)),
            pl.BlockSpec((C_CHUNK, C_GROUPS), lambda i, j: (0, 0)),
            pl.BlockSpec((cw, tn), lambda i, j: (0, j)),
            pl.BlockSpec((tm, tn), lambda i, j: (i, j)),
        ],
        out_specs=pl.BlockSpec((tm, tn), lambda i, j: (i, j)),
        out_shape=jax.ShapeDtypeStruct((t, d), F32),
        scratch_shapes=[pltpu.VMEM((tm, cw), BF16)],
        compiler_params=_params(("arbitrary", "arbitrary")),
    )(p2, p2, p2, lng, lnb, ws, bst, w_out, x2)


def _rope_tables(l):
    t = jnp.arange(l)
    row = (t // GRID_W).astype(F32)
    col = (t % GRID_W).astype(F32)
    n_freq = B_HD // 4
    inv = ROPE_THETA ** (-jnp.arange(n_freq, dtype=F32) / n_freq)
    ar = row[:, None] * inv
    ac = col[:, None] * inv
    cos = jnp.concatenate([jnp.cos(ar), jnp.cos(ar), jnp.cos(ac), jnp.cos(ac)], axis=-1)
    sin = jnp.concatenate([-jnp.sin(ar), jnp.sin(ar), -jnp.sin(ac), jnp.sin(ac)], axis=-1)
    return cos, sin


def _lane_row(v):
    v = v.reshape(1, -1).astype(F32)
    return jnp.pad(v, ((0, 0), (0, LANES - v.shape[1])))


def _pick(n, pref):
    while n % pref:
        pref //= 2
    return pref


def _hybrid_layer(x2, b, l, norm_g, w_in, conv_w, a_log, dt_bias, onorm_g, qn_g, kn_g, w_out):
    t, d = x2.shape
    sizes = (A_W, A_W, A_HEADS * A_DV, A_HEADS * A_DV, 4 * A_HEADS, B_W, B_KV_W, B_KV_W, B_W)
    offs = [0]
    for s in sizes:
        offs.append(offs[-1] + s)
    aq, ak, av, agate, alog_in, bq, bk, bv, bgate = [w_in[:, offs[i]:offs[i + 1]] for i in range(9)]
    w_main = jnp.concatenate([aq, ak, av, agate, bq, bgate, bk, bv], axis=1).astype(BF16)
    w_log = jnp.pad(alog_in, ((0, 0), (0, LANES - alog_in.shape[1]))).astype(BF16)

    tm = _pick(t, 1024)
    p2, lg2 = _proj0(x2, norm_g.reshape(1, d), w_main, w_log, tm, 512)
    p3 = p2.reshape(b, l, P0_COLS)
    lg3 = lg2.reshape(b, l, LANES)

    tb = _pick(l, 512)
    q3, k3, v3, gc3 = _prep(p3, lg3, conv_w.astype(F32), _lane_row(a_log), _lane_row(dt_bias), tb)
    o_f, o_b = _scan(q3, k3, v3, gc3, tb)

    cos, sin_signed = _rope_tables(l)
    yb = _attn(p3, cos, sin_signed, qn_g.reshape(1, B_HD).astype(F32), kn_g.reshape(1, B_HD).astype(F32),
               _pick(l, 256), _pick(l, 512))

    return _out0(o_f.reshape(t, -1), o_b.reshape(t, -1), p2, yb.reshape(t, B_W),
                 onorm_g.reshape(1, A_DV).astype(F32), w_out.astype(BF16), x2, _pick(t, 512), 1024)


def _gmlp_layer(x2, norm_g, w_in, ln_g, ln_b, w_s, b_s, w_out):
    t, d = x2.shape
    cw = w_out.shape[0]
    p2 = _proj1(x2, norm_g.reshape(1, d), w_in.astype(BF16), _pick(t, 1024), 512)
    return _out1(p2, ln_g.reshape(1, cw).astype(F32), ln_b.reshape(1, cw).astype(F32),
                 w_s.astype(BF16), jnp.transpose(b_s).astype(F32), w_out.astype(BF16), x2,
                 _pick(t, 512), 1024)


def kernel(x, norm0_g, w_in0, conv0_w, a_log0, dt_bias0, a_onorm_g0, b_qnorm_g0, b_knorm_g0, w_out0,
           norm1_g, w_in1, c_ln_g1, c_ln_b1, c_ws1, c_bs1, w_out1):
    b, l, d = x.shape
    depth = w_in0.shape[0] + w_in1.shape[0]
    x2 = x.reshape(b * l, d)
    for layer in range(depth):
        i = layer // 2
        if layer % 2 == 0:
            x2 = _hybrid_layer(x2, b, l, norm0_g[i], w_in0[i], conv0_w[i], a_log0[i], dt_bias0[i],
                               a_onorm_g0[i], b_qnorm_g0[i], b_knorm_g0[i], w_out0[i])
        else:
            x2 = _gmlp_layer(x2, norm1_g[i], w_in1[i], c_ln_g1[i], c_ln_b1[i], c_ws1[i], c_bs1[i], w_out1[i])
    return x2.reshape(b, l, d)
```

```python
import functools
import math

import jax
import jax.numpy as jnp
from jax import lax
from jax.experimental import pallas as pl
from jax.experimental.pallas import tpu as pltpu

F32 = jnp.float32
BF16 = jnp.bfloat16

EPS = 1e-6
GRID_W = 64
ROPE_THETA = 10000.0
A_HEADS = 8
A_DK = 128
A_DV = 128
A_CHUNK = 64
CONV_W = 5
A_W = A_HEADS * A_DK
B_HEADS = 8
B_KV_HEADS = 2
B_HD = 128
B_GROUP = B_HEADS // B_KV_HEADS
B_W = B_HEADS * B_HD
B_KV_W = B_KV_HEADS * B_HD
C_GROUPS = 16
C_CHUNK = 128

LANES = 128
VMEM_LIMIT = 56 * 1024 * 1024

COL_AQ = 0
COL_AK = A_W
COL_AV = 2 * A_W
COL_AGATE = 3 * A_W
COL_BQ = 4 * A_W
COL_BGATE = COL_BQ + B_W
COL_BK = COL_BGATE + B_W
COL_BV = COL_BK + B_KV_W
P0_COLS = COL_BV + B_KV_W


def _params(sem):
    return pltpu.CompilerParams(dimension_semantics=sem, vmem_limit_bytes=VMEM_LIMIT)


def _dot(a, b):
    return jnp.dot(a, b, preferred_element_type=F32)


def _dot_nt(a, b):
    return lax.dot_general(a, b, (((1,), (1,)), ((), ())), preferred_element_type=F32)


SUB_ROWS = 256
COL_CHUNK = 512
NORM_ROWS = 64


def _resident(shape):
    nd = len(shape)
    return pl.BlockSpec(shape, lambda *_: (0,) * nd, pipeline_mode=pl.Buffered(1))


def _rmsnorm_rows(x_ref, g_ref, h_ref, row0, nrows):
    for r in range(0, nrows, NORM_ROWS):
        x = x_ref[row0 + r:row0 + r + NORM_ROWS, :]
        ms = jnp.mean(x * x, axis=-1, keepdims=True)
        h_ref[r:r + NORM_ROWS, :] = (x * lax.rsqrt(ms + EPS) * g_ref[...]).astype(BF16)


def _proj0_kernel(x_ref, g_ref, w_ref, wl_ref, p_ref, lg_ref, h_ref):
    n = w_ref.shape[1]
    for sub in range(x_ref.shape[0] // SUB_ROWS):
        h_buf = h_ref.at[sub % 2]
        _rmsnorm_rows(x_ref, g_ref, h_buf, sub * SUB_ROWS, SUB_ROWS)
        rows = slice(sub * SUB_ROWS, (sub + 1) * SUB_ROWS)
        lg_ref[rows, :] = _dot(h_buf[...], wl_ref[...])
        for c0 in range(0, n, COL_CHUNK):
            p_ref[rows, c0:c0 + COL_CHUNK] = _dot(h_buf[...], w_ref[:, c0:c0 + COL_CHUNK]).astype(BF16)


def _proj0(x2, g, w_main, w_log, tm):
    t, d = x2.shape
    n = w_main.shape[1]
    return pl.pallas_call(
        _proj0_kernel,
        grid=(t // tm,),
        in_specs=[
            pl.BlockSpec((tm, d), lambda i: (i, 0)),
            _resident((1, d)),
            _resident((d, n)),
            _resident((d, LANES)),
        ],
        out_specs=[
            pl.BlockSpec((tm, n), lambda i: (i, 0)),
            pl.BlockSpec((tm, LANES), lambda i: (i, 0)),
        ],
        out_shape=[
            jax.ShapeDtypeStruct((t, n), BF16),
            jax.ShapeDtypeStruct((t, LANES), F32),
        ],
        scratch_shapes=[pltpu.VMEM((2, SUB_ROWS, d), BF16)],
        compiler_params=_params(("arbitrary",)),
    )(x2, g, w_main, w_log)


HALO = 16


def _softplus(x):
    return jnp.maximum(x, 0.0) + jnp.log1p(jnp.exp(-jnp.abs(x)))


def _prep_kernel(main_ref, hb_ref, ha_ref, cw_ref, lg_ref, alog_ref, dtb_ref,
                 q_ref, k_ref, v_ref, gc_ref):
    i = pl.program_id(1)
    first = i == 0
    last = i == pl.num_programs(1) - 1
    tb = main_ref.shape[1]
    n_strips = 3 * A_W // LANES
    for s in range(n_strips):
        cs = slice(s * LANES, (s + 1) * LANES)
        xm = main_ref[0, :, cs].astype(F32)
        xb = jnp.where(first, 0.0, hb_ref[0, :, cs].astype(F32))
        xa = jnp.where(last, 0.0, ha_ref[0, :, cs].astype(F32))
        xc = jnp.concatenate([xb, xm, xa], axis=0)
        acc = None
        for j in range(CONV_W):
            shift = (CONV_W // 2 - j) % (tb + 2 * HALO)
            xs = xc if shift == 0 else pltpu.roll(xc, shift, axis=0)
            term = xs[HALO:HALO + tb] * cw_ref[j:j + 1, cs]
            acc = term if acc is None else acc + term
        y = acc * jax.nn.sigmoid(acc)
        head = s % A_HEADS
        hs = slice(head * LANES, (head + 1) * LANES)
        if s < A_HEADS:
            y = y * lax.rsqrt(jnp.sum(y * y, axis=-1, keepdims=True) + EPS) * (A_DK ** -0.5)
            q_ref[0, :, hs] = y.astype(BF16)
        elif s < 2 * A_HEADS:
            y = y * lax.rsqrt(jnp.sum(y * y, axis=-1, keepdims=True) + EPS)
            k_ref[0, :, hs] = y.astype(BF16)
        else:
            v_ref[0, :, hs] = y.astype(BF16)

    lg = lg_ref[0]
    g = -jnp.exp(alog_ref[...]) * _softplus(lg + dtb_ref[...])
    beta = jax.nn.sigmoid(lg)
    row = lax.broadcasted_iota(jnp.int32, (tb, LANES), 0) % A_CHUNK
    lane = lax.broadcasted_iota(jnp.int32, (tb, LANES), 1)
    pf = g
    sf = g
    step = 1
    while step < A_CHUNK:
        pf = pf + jnp.where(row >= step, pltpu.roll(pf, step, axis=0), 0.0)
        sf = sf + jnp.where(row < A_CHUNK - step, pltpu.roll(sf, tb - step, axis=0), 0.0)
        step *= 2
    gc_ref[0] = jnp.where(lane < A_HEADS, pf, jnp.where(lane < 2 * A_HEADS, sf, beta))


def _prep(p3, lg3, conv_w, alog_l, dtb_l, tb):
    b, l, _ = p3.shape
    nb = l // tb
    hpb = tb // HALO
    qkv_shape = jax.ShapeDtypeStruct((b, l, A_W), BF16)
    return pl.pallas_call(
        _prep_kernel,
        grid=(b, nb),
        in_specs=[
            pl.BlockSpec((1, tb, 3 * A_W), lambda bi, i: (bi, i, 0)),
            pl.BlockSpec((1, HALO, 3 * A_W), lambda bi, i: (bi, jnp.maximum(i * hpb - 1, 0), 0)),
            pl.BlockSpec((1, HALO, 3 * A_W),
                         lambda bi, i: (bi, jnp.minimum((i + 1) * hpb, l // HALO - 1), 0)),
            pl.BlockSpec((CONV_W, 3 * A_W), lambda bi, i: (0, 0)),
            pl.BlockSpec((1, tb, LANES), lambda bi, i: (bi, i, 0)),
            pl.BlockSpec((1, LANES), lambda bi, i: (0, 0)),
            pl.BlockSpec((1, LANES), lambda bi, i: (0, 0)),
        ],
        out_specs=[
            pl.BlockSpec((1, tb, A_W), lambda bi, i: (bi, i, 0)),
            pl.BlockSpec((1, tb, A_W), lambda bi, i: (bi, i, 0)),
            pl.BlockSpec((1, tb, A_W), lambda bi, i: (bi, i, 0)),
            pl.BlockSpec((1, tb, LANES), lambda bi, i: (bi, i, 0)),
        ],
        out_shape=[qkv_shape, qkv_shape, qkv_shape, jax.ShapeDtypeStruct((b, l, LANES), F32)],
        compiler_params=_params(("arbitrary", "arbitrary")),
    )(p3, p3, p3, conv_w, lg3, alog_l, dtb_l)


def _delta_chunks(chains):
    c = A_CHUNK
    n = len(chains)
    ri = lax.broadcasted_iota(jnp.int32, (c, c), 0)
    ci = lax.broadcasted_iota(jnp.int32, (c, c), 1)
    decay, eg, kb, k_bf, glast, strict = [], [], [], [], [], []
    for q, k, v, gcol, bcol, grow, s_ref, backward in chains:
        incl = ri <= ci if backward else ri >= ci
        strict.append(ri < ci if backward else ri > ci)
        glast.append(gcol[0:1, :] if backward else gcol[c - 1:c, :])
        decay.append(jnp.where(incl, jnp.exp(gcol - grow), 0.0))
        eg.append(jnp.exp(gcol))
        kb.append(k * bcol)
        k_bf.append(k.astype(BF16))
    kk = [_dot_nt(kb[i].astype(BF16), k_bf[i]) for i in range(n)]
    qk = [_dot_nt(chains[i][0].astype(BF16), k_bf[i]) for i in range(n)]
    nm = [jnp.where(strict[i], -kk[i] * decay[i], 0.0) for i in range(n)]
    rhs = [jnp.concatenate([chains[i][2] * chains[i][4], kb[i] * eg[i]], axis=1) for i in range(n)]
    step = 1
    while True:
        nm_bf = [m.astype(BF16) for m in nm]
        upd = [_dot(nm_bf[i], rhs[i].astype(BF16)) for i in range(n)]
        rhs = [rhs[i] + upd[i] for i in range(n)]
        step *= 2
        if step >= c:
            break
        nm = [_dot(m, m) for m in nm_bf]
    s_old = [ch[6][...] for ch in chains]
    s_bf = [s.astype(BF16) for s in s_old]
    ws = [_dot(rhs[i][:, A_DV:].astype(BF16), s_bf[i]) for i in range(n)]
    qs = [_dot((chains[i][0] * eg[i]).astype(BF16), s_bf[i]) for i in range(n)]
    v_new_bf = [(rhs[i][:, :A_DV] - ws[i]).astype(BF16) for i in range(n)]
    av = [_dot((qk[i] * decay[i]).astype(BF16), v_new_bf[i]) for i in range(n)]
    kd_t = [(chains[i][1] * jnp.exp(glast[i] - chains[i][3])).T.astype(BF16) for i in range(n)]
    kv = [_dot(kd_t[i], v_new_bf[i]) for i in range(n)]
    for i in range(n):
        chains[i][6][...] = s_old[i] * jnp.exp(glast[i]) + kv[i]
    return [qs[i] + av[i] for i in range(n)]


def _scan_kernel(qf_ref, kf_ref, vf_ref, gf_ref, qb_ref, kb_ref, vb_ref, gb_ref,
                 of_ref, ob_ref, s_ref):
    @pl.when(pl.program_id(1) == 0)
    def _():
        s_ref[...] = jnp.zeros_like(s_ref)

    tb = qf_ref.shape[1]
    nc = tb // A_CHUNK

    def body(ci, carry):
        chains, dests = [], []
        for bb in range(qf_ref.shape[0]):
            for backward, (q_ref, k_ref, v_ref, g_ref, o_ref) in enumerate(
                    ((qf_ref, kf_ref, vf_ref, gf_ref, of_ref), (qb_ref, kb_ref, vb_ref, gb_ref, ob_ref))):
                cidx = nc - 1 - ci if backward else ci
                rows = pl.ds(pl.multiple_of(cidx * A_CHUNK, A_CHUNK), A_CHUNK)
                gtile = g_ref[bb, rows, :]
                gt = gtile.T
                for h in range(A_HEADS):
                    hs = slice(h * LANES, (h + 1) * LANES)
                    gl = backward * A_HEADS + h
                    bl = 2 * A_HEADS + gl
                    chains.append((q_ref[bb, rows, hs].astype(F32), k_ref[bb, rows, hs].astype(F32),
                                   v_ref[bb, rows, hs].astype(F32),
                                   gtile[:, gl:gl + 1], gtile[:, bl:bl + 1], gt[gl:gl + 1, :],
                                   s_ref.at[bb * 2 * A_HEADS + gl], bool(backward)))
                    dests.append((o_ref, bb, rows, hs))
        outs = _delta_chunks(chains)
        for (o_ref, bb, rows, hs), o in zip(dests, outs):
            o_ref[bb, rows, hs] = o.astype(BF16)
        return carry

    lax.fori_loop(0, nc, body, 0)


def _scan(q3, k3, v3, gc3, tb, bb):
    b, l, _ = q3.shape
    nb = l // tb
    fwd = lambda bi, i: (bi, i, 0)
    bwd = lambda bi, i: (bi, nb - 1 - i, 0)
    big = lambda m: pl.BlockSpec((bb, tb, A_W), m)
    small = lambda m: pl.BlockSpec((bb, tb, LANES), m)
    o_shape = jax.ShapeDtypeStruct((b, l, A_HEADS * A_DV), BF16)
    return pl.pallas_call(
        _scan_kernel,
        grid=(b // bb, nb),
        in_specs=[big(fwd), big(fwd), big(fwd), small(fwd), big(bwd), big(bwd), big(bwd), small(bwd)],
        out_specs=[big(fwd), big(bwd)],
        out_shape=[o_shape, o_shape],
        scratch_shapes=[pltpu.VMEM((bb * 2 * A_HEADS, A_DK, A_DV), F32)],
        compiler_params=_params(("arbitrary", "arbitrary")),
    )(q3, k3, v3, gc3, q3, k3, v3, gc3)


def _norm_rope(x, g, cos, sin_signed):
    y = x * lax.rsqrt(jnp.mean(x * x, axis=-1, keepdims=True) + EPS) * g
    lane = lax.broadcasted_iota(jnp.int32, y.shape, 1)
    quarter = B_HD // 4
    partner = jnp.where(lane % (2 * quarter) < quarter,
                        pltpu.roll(y, B_HD - quarter, axis=1), pltpu.roll(y, quarter, axis=1))
    return y * cos + partner * sin_signed


def _attn_kernel(q_ref, k_ref, v_ref, gate_ref, cosq_ref, sinq_ref, cosk_ref, sink_ref, qg_ref, kg_ref,
                 o_ref, kr_ref, vx_ref, qs_ref, s_ref, m_ref, acc_ref, *, tk):
    l = k_ref.shape[1]
    tq = q_ref.shape[1]

    @pl.when(pl.program_id(2) == 0)
    def _():
        def kbody(r, carry):
            sl = pl.ds(pl.multiple_of(r * tk, tk), tk)
            kr_ref[sl, :] = _norm_rope(k_ref[0, sl, :].astype(F32), kg_ref[...],
                                       cosk_ref[sl, :], sink_ref[sl, :]).astype(BF16)
            vx_ref[sl, :B_HD] = v_ref[0, sl, :]
            vx_ref[sl, B_HD:] = jnp.ones((tk, B_HD), BF16)
            return carry
        lax.fori_loop(0, l // tk, kbody, 0)

    qscale = (B_HD ** -0.5) * math.log2(math.e)
    for hh in range(B_GROUP):
        qh = _norm_rope(q_ref[0, :, hh * B_HD:(hh + 1) * B_HD].astype(F32), qg_ref[...],
                        cosq_ref[...], sinq_ref[...])
        qs_ref[hh * tq:(hh + 1) * tq, :] = (qh * qscale).astype(BF16)

    m_ref[...] = jnp.full_like(m_ref, -jnp.inf)
    acc_ref[...] = jnp.zeros_like(acc_ref)

    def kv_rows(t):
        if isinstance(t, int):
            return slice(t * tk, (t + 1) * tk)
        return pl.ds(pl.multiple_of(t * tk, tk), tk)

    def scores(t, buf):
        sl = kv_rows(t)
        k_blk = kr_ref[sl, :]
        for hh in range(B_GROUP):
            rows = slice(hh * tq, (hh + 1) * tq)
            s_ref[buf, rows, :] = _dot_nt(qs_ref[rows, :], k_blk)

    def softmax_pv(t, buf):
        sl = kv_rows(t)
        v_blk = vx_ref[sl, :]
        for hh in range(B_GROUP):
            rows = slice(hh * tq, (hh + 1) * tq)
            s = s_ref[buf, rows, :]
            m_prev = m_ref[rows, :]
            m_new = jnp.maximum(m_prev, jnp.max(s, axis=-1, keepdims=True))
            alpha = jnp.exp2(m_prev - m_new)
            p = jnp.concatenate(
                [jnp.exp2(s[:, c * LANES:(c + 1) * LANES] - m_new) for c in range(tk // LANES)],
                axis=1).astype(BF16)
            acc_ref[rows, :] = jnp.concatenate([alpha, alpha], axis=1) * acc_ref[rows, :] + _dot(p, v_blk)
            m_ref[rows, :] = m_new

    n_pairs = l // (2 * tk)
    scores(0, 0)

    def body(i, carry):
        scores(2 * i + 1, 1)
        softmax_pv(2 * i, 0)
        scores(2 * i + 2, 0)
        softmax_pv(2 * i + 1, 1)
        return carry

    lax.fori_loop(0, n_pairs - 1, body, 0)
    scores(2 * n_pairs - 1, 1)
    softmax_pv(2 * n_pairs - 2, 0)
    softmax_pv(2 * n_pairs - 1, 1)

    for hh in range(B_GROUP):
        rows = slice(hh * tq, (hh + 1) * tq)
        o = acc_ref[rows, :B_HD] / acc_ref[rows, B_HD:]
        gate = gate_ref[0, :, hh * B_HD:(hh + 1) * B_HD].astype(F32)
        o_ref[0, :, hh * B_HD:(hh + 1) * B_HD] = (o * (gate * jax.nn.sigmoid(gate))).astype(BF16)


def _attn(p3, cos, sin_signed, qg, kg, tq, tk):
    b, l, _ = p3.shape
    gw = B_GROUP * B_HD
    return pl.pallas_call(
        functools.partial(_attn_kernel, tk=tk),
        grid=(b, B_KV_HEADS, l // tq),
        in_specs=[
            pl.BlockSpec((1, tq, gw), lambda bi, kv, i: (bi, i, COL_BQ // gw + kv)),
            pl.BlockSpec((1, l, B_HD), lambda bi, kv, i: (bi, 0, COL_BK // B_HD + kv)),
            pl.BlockSpec((1, l, B_HD), lambda bi, kv, i: (bi, 0, COL_BV // B_HD + kv)),
            pl.BlockSpec((1, tq, gw), lambda bi, kv, i: (bi, i, COL_BGATE // gw + kv)),
            pl.BlockSpec((tq, B_HD), lambda bi, kv, i: (i, 0)),
            pl.BlockSpec((tq, B_HD), lambda bi, kv, i: (i, 0)),
            pl.BlockSpec((l, B_HD), lambda bi, kv, i: (0, 0)),
            pl.BlockSpec((l, B_HD), lambda bi, kv, i: (0, 0)),
            pl.BlockSpec((1, B_HD), lambda bi, kv, i: (0, 0)),
            pl.BlockSpec((1, B_HD), lambda bi, kv, i: (0, 0)),
        ],
        out_specs=pl.BlockSpec((1, tq, gw), lambda bi, kv, i: (bi, i, kv)),
        out_shape=jax.ShapeDtypeStruct((b, l, B_W), BF16),
        scratch_shapes=[
            pltpu.VMEM((l, B_HD), BF16),
            pltpu.VMEM((l, 2 * B_HD), BF16),
            pltpu.VMEM((B_GROUP * tq, B_HD), BF16),
            pltpu.VMEM((2, B_GROUP * tq, tk), F32),
            pltpu.VMEM((B_GROUP * tq, B_HD), F32),
            pltpu.VMEM((B_GROUP * tq, 2 * B_HD), F32),
        ],
        compiler_params=_params(("arbitrary", "arbitrary", "arbitrary")),
    )(p3, p3, p3, p3, cos, sin_signed, cos, sin_signed, qg, kg)


def _out0_kernel(of_ref, ob_ref, ag_ref, yb_ref, og_ref, w_ref, x_ref, g1_ref, o_ref, h1_ref, y_ref):
    aw = A_HEADS * A_DV
    d = w_ref.shape[1]
    for sub in range(x_ref.shape[0] // SUB_ROWS):
        rows = slice(sub * SUB_ROWS, (sub + 1) * SUB_ROWS)
        y_buf = y_ref.at[sub % 2]
        for h in range(A_HEADS):
            hs = slice(h * A_DV, (h + 1) * A_DV)
            o = of_ref[rows, hs].astype(F32) + ob_ref[rows, hs].astype(F32)
            o = o * lax.rsqrt(jnp.mean(o * o, axis=-1, keepdims=True) + EPS) * og_ref[...]
            gate = ag_ref[rows, hs].astype(F32)
            y_buf[:, hs] = (o * (gate * jax.nn.sigmoid(gate))).astype(BF16)
        y_buf[:, aw:] = yb_ref[rows, :]
        for c0 in range(0, d, COL_CHUNK):
            cs = slice(c0, c0 + COL_CHUNK)
            o_ref[rows, cs] = x_ref[rows, cs] + _dot(y_buf[...], w_ref[:, cs])
        _rmsnorm_rows(o_ref, g1_ref, h1_ref.at[rows], sub * SUB_ROWS, SUB_ROWS)


def _out0(o_f, o_b, p2, yb, og, w_out, x2, g1, tm):
    t, d = x2.shape
    aw = A_HEADS * A_DV
    kdim = aw + B_W
    return pl.pallas_call(
        _out0_kernel,
        grid=(t // tm,),
        in_specs=[
            pl.BlockSpec((tm, aw), lambda i: (i, 0)),
            pl.BlockSpec((tm, aw), lambda i: (i, 0)),
            pl.BlockSpec((tm, aw), lambda i: (i, COL_AGATE // aw)),
            pl.BlockSpec((tm, B_W), lambda i: (i, 0)),
            _resident((1, A_DV)),
            _resident((kdim, d)),
            pl.BlockSpec((tm, d), lambda i: (i, 0)),
            _resident((1, d)),
        ],
        out_specs=[pl.BlockSpec((tm, d), lambda i: (i, 0)), pl.BlockSpec((tm, d), lambda i: (i, 0))],
        out_shape=[jax.ShapeDtypeStruct((t, d), F32), jax.ShapeDtypeStruct((t, d), BF16)],
        scratch_shapes=[pltpu.VMEM((2, SUB_ROWS, kdim), BF16)],
        compiler_params=_params(("arbitrary",)),
    )(o_f, o_b, p2, yb, og, w_out, x2, g1)


def _gelu(x):
    return 0.5 * x * (1.0 + lax.erf(x * (2.0 ** -0.5)))


def _proj1_kernel(h_ref, w_ref, p_ref):
    n = w_ref.shape[1]
    for sub in range(h_ref.shape[0] // SUB_ROWS):
        rows = slice(sub * SUB_ROWS, (sub + 1) * SUB_ROWS)
        for c0 in range(0, n, COL_CHUNK):
            cs = slice(c0, c0 + COL_CHUNK)
            y = _dot(h_ref[rows, :], w_ref[:, cs])
            act = _gelu(y) if c0 < 2 * n // 3 else y * jax.nn.sigmoid(y)
            p_ref[rows, cs] = act.astype(BF16)


def _proj1(h2, w, tm):
    t, d = h2.shape
    n = w.shape[1]
    return pl.pallas_call(
        _proj1_kernel,
        grid=(t // tm,),
        in_specs=[pl.BlockSpec((tm, d), lambda i: (i, 0)), _resident((d, n))],
        out_specs=pl.BlockSpec((tm, n), lambda i: (i, 0)),
        out_shape=jax.ShapeDtypeStruct((t, n), BF16),
        compiler_params=_params(("arbitrary",)),
    )(h2, w)


def _out1_kernel(u_ref, v_ref, gt_ref, lng_ref, lnb_ref, ws_ref, bst_ref, w_ref, x_ref, o_ref, z_ref):
    cw = v_ref.shape[1]
    gw = cw // C_GROUPS
    d = w_ref.shape[1]
    for sub in range(x_ref.shape[0] // SUB_ROWS):
        z_buf = z_ref.at[sub % 2]
        for c in range(SUB_ROWS // C_CHUNK):
            rows = slice(sub * SUB_ROWS + c * C_CHUNK, sub * SUB_ROWS + (c + 1) * C_CHUNK)
            zrows = slice(c * C_CHUNK, (c + 1) * C_CHUNK)
            v = v_ref[rows, :].astype(F32)
            mu = jnp.mean(v, axis=-1, keepdims=True)
            vc = v - mu
            var = jnp.mean(vc * vc, axis=-1, keepdims=True)
            vn = (vc * lax.rsqrt(var + EPS) * lng_ref[...] + lnb_ref[...]).astype(BF16)
            for g in range(C_GROUPS):
                gs = slice(g * gw, (g + 1) * gw)
                s = _dot(ws_ref[g], vn[:, gs]) + bst_ref[:, g:g + 1]
                z = u_ref[rows, gs].astype(F32) * s * gt_ref[rows, gs].astype(F32)
                z_buf[zrows, gs] = z.astype(BF16)
        rows = slice(sub * SUB_ROWS, (sub + 1) * SUB_ROWS)
        for c0 in range(0, d, COL_CHUNK):
            cs = slice(c0, c0 + COL_CHUNK)
            o_ref[rows, cs] = x_ref[rows, cs] + _dot(z_buf[...], w_ref[:, cs])


def _out1(p2, lng, lnb, ws, bst, w_out, x2, tm):
    t, d = x2.shape
    cw = w_out.shape[0]
    return pl.pallas_call(
        _out1_kernel,
        grid=(t // tm,),
        in_specs=[
            pl.BlockSpec((tm, cw), lambda i: (i, 0)),
            pl.BlockSpec((tm, cw), lambda i: (i, 1)),
            pl.BlockSpec((tm, cw), lambda i: (i, 2)),
            _resident((1, cw)),
            _resident((1, cw)),
            _resident((C_GROUPS, C_CHUNK, C_CHUNK)),
            _resident((C_CHUNK, C_GROUPS)),
            _resident((cw, d)),
            pl.BlockSpec((tm, d), lambda i: (i, 0)),
        ],
        out_specs=pl.BlockSpec((tm, d), lambda i: (i, 0)),
        out_shape=jax.ShapeDtypeStruct((t, d), F32),
        scratch_shapes=[pltpu.VMEM((2, SUB_ROWS, cw), BF16)],
        compiler_params=_params(("arbitrary",)),
    )(p2, p2, p2, lng, lnb, ws, bst, w_out, x2)


def _rope_tables(l):
    t = jnp.arange(l)
    row = (t // GRID_W).astype(F32)
    col = (t % GRID_W).astype(F32)
    n_freq = B_HD // 4
    inv = ROPE_THETA ** (-jnp.arange(n_freq, dtype=F32) / n_freq)
    ar = row[:, None] * inv
    ac = col[:, None] * inv
    cos = jnp.concatenate([jnp.cos(ar), jnp.cos(ar), jnp.cos(ac), jnp.cos(ac)], axis=-1)
    sin = jnp.concatenate([-jnp.sin(ar), jnp.sin(ar), -jnp.sin(ac), jnp.sin(ac)], axis=-1)
    return cos, sin


def _lane_row(v):
    v = v.reshape(1, -1).astype(F32)
    return jnp.pad(v, ((0, 0), (0, LANES - v.shape[1])))


def _pick(n, pref):
    while n % pref:
        pref //= 2
    return pref


def _hybrid_layer(x2, b, l, norm_g, w_in, conv_w, a_log, dt_bias, onorm_g, qn_g, kn_g, w_out, next_norm_g):
    t, d = x2.shape
    sizes = (A_W, A_W, A_HEADS * A_DV, A_HEADS * A_DV, 4 * A_HEADS, B_W, B_KV_W, B_KV_W, B_W)
    offs = [0]
    for s in sizes:
        offs.append(offs[-1] + s)
    aq, ak, av, agate, alog_in, bq, bk, bv, bgate = [w_in[:, offs[i]:offs[i + 1]] for i in range(9)]
    w_main = jnp.concatenate([aq, ak, av, agate, bq, bgate, bk, bv], axis=1).astype(BF16)
    w_log = jnp.pad(alog_in, ((0, 0), (0, LANES - alog_in.shape[1]))).astype(BF16)

    tm = _pick(t, 512)
    p2, lg2 = _proj0(x2, norm_g.reshape(1, d), w_main, w_log, tm)
    p3 = p2.reshape(b, l, P0_COLS)
    lg3 = lg2.reshape(b, l, LANES)

    tb = _pick(l, 512)
    q3, k3, v3, gc3 = _prep(p3, lg3, conv_w.astype(F32), _lane_row(a_log), _lane_row(dt_bias), tb)
    o_f, o_b = _scan(q3, k3, v3, gc3, tb, 2 if b % 2 == 0 else 1)

    cos, sin_signed = _rope_tables(l)
    yb = _attn(p3, cos, sin_signed, qn_g.reshape(1, B_HD).astype(F32), kn_g.reshape(1, B_HD).astype(F32),
               _pick(l, 256), _pick(l, 512))

    return _out0(o_f.reshape(t, -1), o_b.reshape(t, -1), p2, yb.reshape(t, B_W),
                 onorm_g.reshape(1, A_DV).astype(F32), w_out.astype(BF16), x2,
                 next_norm_g.reshape(1, d).astype(F32), tm)


def _gmlp_layer(x2, h2, w_in, ln_g, ln_b, w_s, b_s, w_out):
    t, d = x2.shape
    cw = w_out.shape[0]
    tm = _pick(t, 512)
    p2 = _proj1(h2, w_in.astype(BF16), tm)
    return _out1(p2, ln_g.reshape(1, cw).astype(F32), ln_b.reshape(1, cw).astype(F32),
                 w_s.astype(BF16), jnp.transpose(b_s).astype(F32), w_out.astype(BF16), x2, tm)


def kernel(x, norm0_g, w_in0, conv0_w, a_log0, dt_bias0, a_onorm_g0, b_qnorm_g0, b_knorm_g0, w_out0,
           norm1_g, w_in1, c_ln_g1, c_ln_b1, c_ws1, c_bs1, w_out1):
    b, l, d = x.shape
    assert w_in0.shape[0] == 1 and w_in1.shape[0] == 1
    x2 = x.reshape(b * l, d)
    x2, h2 = _hybrid_layer(x2, b, l, norm0_g[0], w_in0[0], conv0_w[0], a_log0[0], dt_bias0[0],
                           a_onorm_g0[0], b_qnorm_g0[0], b_knorm_g0[0], w_out0[0], norm1_g[0])
    x2 = _gmlp_layer(x2, h2, w_in1[0], c_ln_g1[0], c_ln_b1[0], c_ws1[0], c_bs1[0], w_out1[0])
    return x2.reshape(b, l, d)
```

```python
import functools
import math

import jax
import jax.numpy as jnp
from jax import lax
from jax.experimental import pallas as pl
from jax.experimental.pallas import tpu as pltpu

F32 = jnp.float32
BF16 = jnp.bfloat16

EPS = 1e-6
GRID_W = 64
ROPE_THETA = 10000.0
A_HEADS = 8
A_DK = 128
A_DV = 128
A_CHUNK = 64
CONV_W = 5
A_W = A_HEADS * A_DK
B_HEADS = 8
B_KV_HEADS = 2
B_HD = 128
B_GROUP = B_HEADS // B_KV_HEADS
B_W = B_HEADS * B_HD
B_KV_W = B_KV_HEADS * B_HD
C_GROUPS = 16
C_CHUNK = 128

LANES = 128
VMEM_LIMIT = 56 * 1024 * 1024

COL_AQ = 0
COL_AK = A_W
COL_AV = 2 * A_W
COL_AGATE = 3 * A_W
COL_BQ = 4 * A_W
COL_BGATE = COL_BQ + B_W
COL_BK = COL_BGATE + B_W
COL_BV = COL_BK + B_KV_W
P0_COLS = COL_BV + B_KV_W


def _params(sem):
    return pltpu.CompilerParams(dimension_semantics=sem, vmem_limit_bytes=VMEM_LIMIT)


def _dot(a, b):
    return jnp.dot(a, b, preferred_element_type=F32)


def _dot_nt(a, b):
    return lax.dot_general(a, b, (((1,), (1,)), ((), ())), preferred_element_type=F32)


SUB_ROWS = 256
COL_CHUNK = 512
NORM_ROWS = 64


def _resident(shape):
    nd = len(shape)
    return pl.BlockSpec(shape, lambda *_: (0,) * nd, pipeline_mode=pl.Buffered(1))


def _rmsnorm_rows(x_ref, g_ref, h_ref, row0, nrows):
    for r in range(0, nrows, NORM_ROWS):
        x = x_ref[row0 + r:row0 + r + NORM_ROWS, :]
        ms = jnp.mean(x * x, axis=-1, keepdims=True)
        h_ref[r:r + NORM_ROWS, :] = (x * lax.rsqrt(ms + EPS) * g_ref[...]).astype(BF16)


def _silu(x):
    return x * jax.nn.sigmoid(x)


def _norm_rope(x, g, cos, sin_signed):
    y = x * lax.rsqrt(jnp.mean(x * x, axis=-1, keepdims=True) + EPS) * g
    lane = lax.broadcasted_iota(jnp.int32, y.shape, 1)
    quarter = B_HD // 4
    partner = jnp.where(lane % (2 * quarter) < quarter,
                        pltpu.roll(y, B_HD - quarter, axis=1), pltpu.roll(y, quarter, axis=1))
    return y * cos + partner * sin_signed


Q_SCALE = (B_HD ** -0.5) * math.log2(math.e)


def _proj0_epilogue(y, c0, cos, sin_signed, qg, kg):
    if COL_AGATE <= c0 < COL_BQ or COL_BGATE <= c0 < COL_BK:
        return _silu(y)
    if COL_BQ <= c0 < COL_BGATE or COL_BK <= c0 < COL_BV:
        heads = []
        for h0 in range(0, COL_CHUNK, B_HD):
            yh = y[:, h0:h0 + B_HD]
            if COL_BQ <= c0 + h0 < COL_BGATE:
                yh = _norm_rope(yh, qg, cos, sin_signed) * Q_SCALE
            elif COL_BK <= c0 + h0 < COL_BV:
                yh = _norm_rope(yh, kg, cos, sin_signed)
            heads.append(yh)
        return jnp.concatenate(heads, axis=1)
    return y


def _proj0_kernel(x_ref, g_ref, w_ref, wl_ref, cos_ref, sin_ref, qg_ref, kg_ref, p_ref, lg_ref, h_ref):
    n = w_ref.shape[1]
    for sub in range(x_ref.shape[0] // SUB_ROWS):
        h_buf = h_ref.at[sub % 2]
        _rmsnorm_rows(x_ref, g_ref, h_buf, sub * SUB_ROWS, SUB_ROWS)
        rows = slice(sub * SUB_ROWS, (sub + 1) * SUB_ROWS)
        lg_ref[rows, :] = _dot(h_buf[...], wl_ref[...])
        for c0 in range(0, n, COL_CHUNK):
            y = _dot(h_buf[...], w_ref[:, c0:c0 + COL_CHUNK])
            y = _proj0_epilogue(y, c0, cos_ref[rows, :], sin_ref[rows, :], qg_ref[...], kg_ref[...])
            p_ref[rows, c0:c0 + COL_CHUNK] = y.astype(BF16)


def _proj0(x2, g, w_main, w_log, cos, sin_signed, qg, kg, tm):
    t, d = x2.shape
    n = w_main.shape[1]
    l = cos.shape[0]
    assert l % tm == 0
    return pl.pallas_call(
        _proj0_kernel,
        grid=(t // tm,),
        in_specs=[
            pl.BlockSpec((tm, d), lambda i: (i, 0)),
            _resident((1, d)),
            _resident((d, n)),
            _resident((d, LANES)),
            pl.BlockSpec((tm, B_HD), lambda i: (i % (l // tm), 0)),
            pl.BlockSpec((tm, B_HD), lambda i: (i % (l // tm), 0)),
            _resident((1, B_HD)),
            _resident((1, B_HD)),
        ],
        out_specs=[
            pl.BlockSpec((tm, n), lambda i: (i, 0)),
            pl.BlockSpec((tm, LANES), lambda i: (i, 0)),
        ],
        out_shape=[
            jax.ShapeDtypeStruct((t, n), BF16),
            jax.ShapeDtypeStruct((t, LANES), F32),
        ],
        scratch_shapes=[pltpu.VMEM((2, SUB_ROWS, d), BF16)],
        compiler_params=_params(("arbitrary",)),
    )(x2, g, w_main, w_log, cos, sin_signed, qg, kg)


HALO = 16


def _softplus(x):
    return jnp.maximum(x, 0.0) + jnp.log1p(jnp.exp(-jnp.abs(x)))


def _prep_kernel(main_ref, hb_ref, ha_ref, cw_ref, lg_ref, alog_ref, dtb_ref,
                 q_ref, k_ref, v_ref, gc_ref):
    i = pl.program_id(1)
    first = i == 0
    last = i == pl.num_programs(1) - 1
    tb = main_ref.shape[1]
    n_strips = 3 * A_W // LANES
    for s in range(n_strips):
        cs = slice(s * LANES, (s + 1) * LANES)
        xm = main_ref[0, :, cs].astype(F32)
        xb = jnp.where(first, 0.0, hb_ref[0, :, cs].astype(F32))
        xa = jnp.where(last, 0.0, ha_ref[0, :, cs].astype(F32))
        xc = jnp.concatenate([xb, xm, xa], axis=0)
        acc = None
        for j in range(CONV_W):
            shift = (CONV_W // 2 - j) % (tb + 2 * HALO)
            xs = xc if shift == 0 else pltpu.roll(xc, shift, axis=0)
            term = xs[HALO:HALO + tb] * cw_ref[j:j + 1, cs]
            acc = term if acc is None else acc + term
        y = acc * jax.nn.sigmoid(acc)
        head = s % A_HEADS
        hs = slice(head * LANES, (head + 1) * LANES)
        if s < A_HEADS:
            y = y * lax.rsqrt(jnp.sum(y * y, axis=-1, keepdims=True) + EPS) * (A_DK ** -0.5)
            q_ref[0, :, hs] = y.astype(BF16)
        elif s < 2 * A_HEADS:
            y = y * lax.rsqrt(jnp.sum(y * y, axis=-1, keepdims=True) + EPS)
            k_ref[0, :, hs] = y.astype(BF16)
        else:
            v_ref[0, :, hs] = y.astype(BF16)

    lg = lg_ref[0]
    g = -jnp.exp(alog_ref[...]) * _softplus(lg + dtb_ref[...])
    beta = jax.nn.sigmoid(lg)
    row = lax.broadcasted_iota(jnp.int32, (tb, LANES), 0) % A_CHUNK
    lane = lax.broadcasted_iota(jnp.int32, (tb, LANES), 1)
    pf = g
    sf = g
    step = 1
    while step < A_CHUNK:
        pf = pf + jnp.where(row >= step, pltpu.roll(pf, step, axis=0), 0.0)
        sf = sf + jnp.where(row < A_CHUNK - step, pltpu.roll(sf, tb - step, axis=0), 0.0)
        step *= 2
    gc_ref[0] = jnp.where(lane < A_HEADS, pf, jnp.where(lane < 2 * A_HEADS, sf, beta))


def _prep(p3, lg3, conv_w, alog_l, dtb_l, tb):
    b, l, _ = p3.shape
    nb = l // tb
    hpb = tb // HALO
    qkv_shape = jax.ShapeDtypeStruct((b, l, A_W), BF16)
    return pl.pallas_call(
        _prep_kernel,
        grid=(b, nb),
        in_specs=[
            pl.BlockSpec((1, tb, 3 * A_W), lambda bi, i: (bi, i, 0)),
            pl.BlockSpec((1, HALO, 3 * A_W), lambda bi, i: (bi, jnp.maximum(i * hpb - 1, 0), 0)),
            pl.BlockSpec((1, HALO, 3 * A_W),
                         lambda bi, i: (bi, jnp.minimum((i + 1) * hpb, l // HALO - 1), 0)),
            pl.BlockSpec((CONV_W, 3 * A_W), lambda bi, i: (0, 0)),
            pl.BlockSpec((1, tb, LANES), lambda bi, i: (bi, i, 0)),
            pl.BlockSpec((1, LANES), lambda bi, i: (0, 0)),
            pl.BlockSpec((1, LANES), lambda bi, i: (0, 0)),
        ],
        out_specs=[
            pl.BlockSpec((1, tb, A_W), lambda bi, i: (bi, i, 0)),
            pl.BlockSpec((1, tb, A_W), lambda bi, i: (bi, i, 0)),
            pl.BlockSpec((1, tb, A_W), lambda bi, i: (bi, i, 0)),
            pl.BlockSpec((1, tb, LANES), lambda bi, i: (bi, i, 0)),
        ],
        out_shape=[qkv_shape, qkv_shape, qkv_shape, jax.ShapeDtypeStruct((b, l, LANES), F32)],
        compiler_params=_params(("arbitrary", "arbitrary")),
    )(p3, p3, p3, conv_w, lg3, alog_l, dtb_l)


def _delta_chunks(chains):
    c = A_CHUNK
    n = len(chains)
    ri = lax.broadcasted_iota(jnp.int32, (c, c), 0)
    ci = lax.broadcasted_iota(jnp.int32, (c, c), 1)
    decay, eg, kb, k_bf, glast, strict = [], [], [], [], [], []
    for q, k, v, gcol, bcol, grow, s_ref, backward in chains:
        incl = ri <= ci if backward else ri >= ci
        strict.append(ri < ci if backward else ri > ci)
        glast.append(gcol[0:1, :] if backward else gcol[c - 1:c, :])
        decay.append(jnp.where(incl, jnp.exp(gcol - grow), 0.0))
        eg.append(jnp.exp(gcol))
        kb.append(k * bcol)
        k_bf.append(k.astype(BF16))
    kk = [_dot_nt(kb[i].astype(BF16), k_bf[i]) for i in range(n)]
    qk = [_dot_nt(chains[i][0].astype(BF16), k_bf[i]) for i in range(n)]
    nm = [jnp.where(strict[i], -kk[i] * decay[i], 0.0) for i in range(n)]
    eye = (ri == ci).astype(F32)
    t_inv = [eye + m for m in nm]
    step = 2
    while step < c:
        nm_bf = [m.astype(BF16) for m in nm]
        nm = [_dot(m, m) for m in nm_bf]
        upd = [_dot(nm[i].astype(BF16), t_inv[i].astype(BF16)) for i in range(n)]
        t_inv = [t_inv[i] + upd[i] for i in range(n)]
        step *= 2
    rhs = [_dot(t_inv[i].astype(BF16),
                jnp.concatenate([chains[i][2] * chains[i][4], kb[i] * eg[i]], axis=1).astype(BF16))
           for i in range(n)]
    s_old = [ch[6][...] for ch in chains]
    s_bf = [s.astype(BF16) for s in s_old]
    ws = [_dot(rhs[i][:, A_DV:].astype(BF16), s_bf[i]) for i in range(n)]
    qs = [_dot((chains[i][0] * eg[i]).astype(BF16), s_bf[i]) for i in range(n)]
    v_new_bf = [(rhs[i][:, :A_DV] - ws[i]).astype(BF16) for i in range(n)]
    av = [_dot((qk[i] * decay[i]).astype(BF16), v_new_bf[i]) for i in range(n)]
    kd_t = [(chains[i][1] * jnp.exp(glast[i] - chains[i][3])).T.astype(BF16) for i in range(n)]
    kv = [_dot(kd_t[i], v_new_bf[i]) for i in range(n)]
    for i in range(n):
        chains[i][6][...] = s_old[i] * jnp.exp(glast[i]) + kv[i]
    return [qs[i] + av[i] for i in range(n)]


def _scan_kernel(qf_ref, kf_ref, vf_ref, gf_ref, qb_ref, kb_ref, vb_ref, gb_ref,
                 of_ref, ob_ref, s_ref):
    @pl.when(pl.program_id(1) == 0)
    def _():
        s_ref[...] = jnp.zeros_like(s_ref)

    tb = qf_ref.shape[1]
    nc = tb // A_CHUNK

    def body(ci, carry):
        chains, dests = [], []
        for bb in range(qf_ref.shape[0]):
            for backward, (q_ref, k_ref, v_ref, g_ref, o_ref) in enumerate(
                    ((qf_ref, kf_ref, vf_ref, gf_ref, of_ref), (qb_ref, kb_ref, vb_ref, gb_ref, ob_ref))):
                cidx = nc - 1 - ci if backward else ci
                rows = pl.ds(pl.multiple_of(cidx * A_CHUNK, A_CHUNK), A_CHUNK)
                gtile = g_ref[bb, rows, :]
                gt = gtile.T
                for h in range(A_HEADS):
                    hs = slice(h * LANES, (h + 1) * LANES)
                    gl = backward * A_HEADS + h
                    bl = 2 * A_HEADS + gl
                    chains.append((q_ref[bb, rows, hs].astype(F32), k_ref[bb, rows, hs].astype(F32),
                                   v_ref[bb, rows, hs].astype(F32),
                                   gtile[:, gl:gl + 1], gtile[:, bl:bl + 1], gt[gl:gl + 1, :],
                                   s_ref.at[bb * 2 * A_HEADS + gl], bool(backward)))
                    dests.append((o_ref, bb, rows, hs))
        outs = _delta_chunks(chains)
        for (o_ref, bb, rows, hs), o in zip(dests, outs):
            o_ref[bb, rows, hs] = o.astype(BF16)
        return carry

    lax.fori_loop(0, nc, body, 0)


def _scan(q3, k3, v3, gc3, tb, bb):
    b, l, _ = q3.shape
    nb = l // tb
    fwd = lambda bi, i: (bi, i, 0)
    bwd = lambda bi, i: (bi, nb - 1 - i, 0)
    big = lambda m: pl.BlockSpec((bb, tb, A_W), m)
    small = lambda m: pl.BlockSpec((bb, tb, LANES), m)
    o_shape = jax.ShapeDtypeStruct((b, l, A_HEADS * A_DV), BF16)
    return pl.pallas_call(
        _scan_kernel,
        grid=(b // bb, nb),
        in_specs=[big(fwd), big(fwd), big(fwd), small(fwd), big(bwd), big(bwd), big(bwd), small(bwd)],
        out_specs=[big(fwd), big(bwd)],
        out_shape=[o_shape, o_shape],
        scratch_shapes=[pltpu.VMEM((bb * 2 * A_HEADS, A_DK, A_DV), F32)],
        compiler_params=_params(("arbitrary", "arbitrary")),
    )(q3, k3, v3, gc3, q3, k3, v3, gc3)


def _attn_kernel(q_ref, qn_ref, k_ref, v_ref, gate_ref, o_ref, vx_ref, s_ref, m_ref, acc_ref, *, tk):
    l = k_ref.shape[1]
    tq = q_ref.shape[1]

    def kv_rows(t):
        if isinstance(t, int):
            return slice(t * tk, (t + 1) * tk)
        return pl.ds(pl.multiple_of(t * tk, tk), tk)

    def scores(src_ref, t, buf):
        k_blk = k_ref[0, kv_rows(t), :]
        for hh in range(B_GROUP):
            rows = slice(hh * tq, (hh + 1) * tq)
            s_ref[buf, rows, :] = _dot_nt(src_ref[0, :, hh * B_HD:(hh + 1) * B_HD], k_blk)

    @pl.when(pl.program_id(2) == 0)
    def _():
        def vbody(r, carry):
            sl = kv_rows(r)
            vx_ref[sl, :B_HD] = v_ref[0, sl, :]
            vx_ref[sl, B_HD:] = jnp.ones((tk, B_HD), BF16)
            return carry
        lax.fori_loop(0, l // tk, vbody, 0)
        scores(q_ref, 0, 0)

    m_ref[...] = jnp.full_like(m_ref, -jnp.inf)
    acc_ref[...] = jnp.zeros_like(acc_ref)

    def softmax_pv(t, buf):
        v_blk = vx_ref[kv_rows(t), :]
        for hh in range(B_GROUP):
            rows = slice(hh * tq, (hh + 1) * tq)
            s = s_ref[buf, rows, :]
            m_prev = m_ref[rows, :]
            m_new = jnp.maximum(m_prev, jnp.max(s, axis=-1, keepdims=True))
            alpha = jnp.exp2(m_prev - m_new)
            p = jnp.concatenate(
                [jnp.exp2(s[:, c * LANES:(c + 1) * LANES] - m_new) for c in range(tk // LANES)],
                axis=1).astype(BF16)
            acc_ref[rows, :] = jnp.concatenate([alpha, alpha], axis=1) * acc_ref[rows, :] + _dot(p, v_blk)
            m_ref[rows, :] = m_new

    n_pairs = l // (2 * tk)

    def body(i, carry):
        scores(q_ref, 2 * i + 1, 1)
        softmax_pv(2 * i, 0)
        scores(q_ref, 2 * i + 2, 0)
        softmax_pv(2 * i + 1, 1)
        return carry

    lax.fori_loop(0, n_pairs - 1, body, 0)
    scores(q_ref, 2 * n_pairs - 1, 1)
    softmax_pv(2 * n_pairs - 2, 0)
    scores(qn_ref, 0, 0)
    softmax_pv(2 * n_pairs - 1, 1)

    for hh in range(B_GROUP):
        rows = slice(hh * tq, (hh + 1) * tq)
        o = acc_ref[rows, :B_HD] / acc_ref[rows, B_HD:]
        gate = gate_ref[0, :, hh * B_HD:(hh + 1) * B_HD].astype(F32)
        o_ref[0, :, hh * B_HD:(hh + 1) * B_HD] = (o * gate).astype(BF16)


def _attn(p3, tq, tk):
    b, l, _ = p3.shape
    gw = B_GROUP * B_HD
    nq = l // tq
    return pl.pallas_call(
        functools.partial(_attn_kernel, tk=tk),
        grid=(b, B_KV_HEADS, nq),
        in_specs=[
            pl.BlockSpec((1, tq, gw), lambda bi, kv, i: (bi, i, COL_BQ // gw + kv)),
            pl.BlockSpec((1, tq, gw), lambda bi, kv, i: (bi, jnp.minimum(i + 1, nq - 1), COL_BQ // gw + kv)),
            pl.BlockSpec((1, l, B_HD), lambda bi, kv, i: (bi, 0, COL_BK // B_HD + kv)),
            pl.BlockSpec((1, l, B_HD), lambda bi, kv, i: (bi, 0, COL_BV // B_HD + kv)),
            pl.BlockSpec((1, tq, gw), lambda bi, kv, i: (bi, i, COL_BGATE // gw + kv)),
        ],
        out_specs=pl.BlockSpec((1, tq, gw), lambda bi, kv, i: (bi, i, kv)),
        out_shape=jax.ShapeDtypeStruct((b, l, B_W), BF16),
        scratch_shapes=[
            pltpu.VMEM((l, 2 * B_HD), BF16),
            pltpu.VMEM((2, B_GROUP * tq, tk), F32),
            pltpu.VMEM((B_GROUP * tq, B_HD), F32),
            pltpu.VMEM((B_GROUP * tq, 2 * B_HD), F32),
        ],
        compiler_params=_params(("arbitrary", "arbitrary", "arbitrary")),
    )(p3, p3, p3, p3, p3)


def _out0_kernel(of_ref, ob_ref, ag_ref, yb_ref, og_ref, w_ref, x_ref, g1_ref, o_ref, h1_ref, y_ref):
    aw = A_HEADS * A_DV
    d = w_ref.shape[1]
    for sub in range(x_ref.shape[0] // SUB_ROWS):
        rows = slice(sub * SUB_ROWS, (sub + 1) * SUB_ROWS)
        y_buf = y_ref.at[sub % 2]
        for h in range(A_HEADS):
            hs = slice(h * A_DV, (h + 1) * A_DV)
            o = of_ref[rows, hs].astype(F32) + ob_ref[rows, hs].astype(F32)
            o = o * lax.rsqrt(jnp.mean(o * o, axis=-1, keepdims=True) + EPS) * og_ref[...]
            y_buf[:, hs] = (o * ag_ref[rows, hs].astype(F32)).astype(BF16)
        y_buf[:, aw:] = yb_ref[rows, :]
        for c0 in range(0, d, COL_CHUNK):
            cs = slice(c0, c0 + COL_CHUNK)
            o_ref[rows, cs] = x_ref[rows, cs] + _dot(y_buf[...], w_ref[:, cs])
        _rmsnorm_rows(o_ref, g1_ref, h1_ref.at[rows], sub * SUB_ROWS, SUB_ROWS)


def _out0(o_f, o_b, p2, yb, og, w_out, x2, g1, tm):
    t, d = x2.shape
    aw = A_HEADS * A_DV
    kdim = aw + B_W
    return pl.pallas_call(
        _out0_kernel,
        grid=(t // tm,),
        in_specs=[
            pl.BlockSpec((tm, aw), lambda i: (i, 0)),
            pl.BlockSpec((tm, aw), lambda i: (i, 0)),
            pl.BlockSpec((tm, aw), lambda i: (i, COL_AGATE // aw)),
            pl.BlockSpec((tm, B_W), lambda i: (i, 0)),
            _resident((1, A_DV)),
            _resident((kdim, d)),
            pl.BlockSpec((tm, d), lambda i: (i, 0)),
            _resident((1, d)),
        ],
        out_specs=[pl.BlockSpec((tm, d), lambda i: (i, 0)), pl.BlockSpec((tm, d), lambda i: (i, 0))],
        out_shape=[jax.ShapeDtypeStruct((t, d), F32), jax.ShapeDtypeStruct((t, d), BF16)],
        scratch_shapes=[pltpu.VMEM((2, SUB_ROWS, kdim), BF16)],
        compiler_params=_params(("arbitrary",)),
    )(o_f, o_b, p2, yb, og, w_out, x2, g1)


def _gelu(x):
    return 0.5 * x * (1.0 + lax.erf(x * (2.0 ** -0.5)))


def _proj1_kernel(h_ref, w_ref, p_ref):
    n = w_ref.shape[1]
    for sub in range(h_ref.shape[0] // SUB_ROWS):
        rows = slice(sub * SUB_ROWS, (sub + 1) * SUB_ROWS)
        for c0 in range(0, n, COL_CHUNK):
            cs = slice(c0, c0 + COL_CHUNK)
            y = _dot(h_ref[rows, :], w_ref[:, cs])
            act = _gelu(y) if c0 < 2 * n // 3 else y * jax.nn.sigmoid(y)
            p_ref[rows, cs] = act.astype(BF16)


def _proj1(h2, w, tm):
    t, d = h2.shape
    n = w.shape[1]
    return pl.pallas_call(
        _proj1_kernel,
        grid=(t // tm,),
        in_specs=[pl.BlockSpec((tm, d), lambda i: (i, 0)), _resident((d, n))],
        out_specs=pl.BlockSpec((tm, n), lambda i: (i, 0)),
        out_shape=jax.ShapeDtypeStruct((t, n), BF16),
        compiler_params=_params(("arbitrary",)),
    )(h2, w)


def _out1_kernel(u_ref, v_ref, gt_ref, lng_ref, lnb_ref, ws_ref, bst_ref, w_ref, x_ref, o_ref, z_ref):
    cw = v_ref.shape[1]
    gw = cw // C_GROUPS
    d = w_ref.shape[1]
    for sub in range(x_ref.shape[0] // SUB_ROWS):
        z_buf = z_ref.at[sub % 2]
        for c in range(SUB_ROWS // C_CHUNK):
            rows = slice(sub * SUB_ROWS + c * C_CHUNK, sub * SUB_ROWS + (c + 1) * C_CHUNK)
            zrows = slice(c * C_CHUNK, (c + 1) * C_CHUNK)
            v = v_ref[rows, :].astype(F32)
            mu = jnp.mean(v, axis=-1, keepdims=True)
            vc = v - mu
            var = jnp.mean(vc * vc, axis=-1, keepdims=True)
            vn = (vc * lax.rsqrt(var + EPS) * lng_ref[...] + lnb_ref[...]).astype(BF16)
            for g in range(C_GROUPS):
                gs = slice(g * gw, (g + 1) * gw)
                s = _dot(ws_ref[g], vn[:, gs]) + bst_ref[:, g:g + 1]
                z = u_ref[rows, gs].astype(F32) * s * gt_ref[rows, gs].astype(F32)
                z_buf[zrows, gs] = z.astype(BF16)
        rows = slice(sub * SUB_ROWS, (sub + 1) * SUB_ROWS)
        for c0 in range(0, d, COL_CHUNK):
            cs = slice(c0, c0 + COL_CHUNK)
            o_ref[rows, cs] = x_ref[rows, cs] + _dot(z_buf[...], w_ref[:, cs])


def _out1(p2, lng, lnb, ws, bst, w_out, x2, tm):
    t, d = x2.shape
    cw = w_out.shape[0]
    return pl.pallas_call(
        _out1_kernel,
        grid=(t // tm,),
        in_specs=[
            pl.BlockSpec((tm, cw), lambda i: (i, 0)),
            pl.BlockSpec((tm, cw), lambda i: (i, 1)),
            pl.BlockSpec((tm, cw), lambda i: (i, 2)),
            _resident((1, cw)),
            _resident((1, cw)),
            _resident((C_GROUPS, C_CHUNK, C_CHUNK)),
            _resident((C_CHUNK, C_GROUPS)),
            _resident((cw, d)),
            pl.BlockSpec((tm, d), lambda i: (i, 0)),
        ],
        out_specs=pl.BlockSpec((tm, d), lambda i: (i, 0)),
        out_shape=jax.ShapeDtypeStruct((t, d), F32),
        scratch_shapes=[pltpu.VMEM((2, SUB_ROWS, cw), BF16)],
        compiler_params=_params(("arbitrary",)),
    )(p2, p2, p2, lng, lnb, ws, bst, w_out, x2)


def _rope_tables(l):
    t = jnp.arange(l)
    row = (t // GRID_W).astype(F32)
    col = (t % GRID_W).astype(F32)
    n_freq = B_HD // 4
    inv = ROPE_THETA ** (-jnp.arange(n_freq, dtype=F32) / n_freq)
    ar = row[:, None] * inv
    ac = col[:, None] * inv
    cos = jnp.concatenate([jnp.cos(ar), jnp.cos(ar), jnp.cos(ac), jnp.cos(ac)], axis=-1)
    sin = jnp.concatenate([-jnp.sin(ar), jnp.sin(ar), -jnp.sin(ac), jnp.sin(ac)], axis=-1)
    return cos, sin


def _lane_row(v):
    v = v.reshape(1, -1).astype(F32)
    return jnp.pad(v, ((0, 0), (0, LANES - v.shape[1])))


def _pick(n, pref):
    while n % pref:
        pref //= 2
    return pref


def _hybrid_layer(x2, b, l, norm_g, w_in, conv_w, a_log, dt_bias, onorm_g, qn_g, kn_g, w_out, next_norm_g):
    t, d = x2.shape
    sizes = (A_W, A_W, A_HEADS * A_DV, A_HEADS * A_DV, 4 * A_HEADS, B_W, B_KV_W, B_KV_W, B_W)
    offs = [0]
    for s in sizes:
        offs.append(offs[-1] + s)
    aq, ak, av, agate, alog_in, bq, bk, bv, bgate = [w_in[:, offs[i]:offs[i + 1]] for i in range(9)]
    w_main = jnp.concatenate([aq, ak, av, agate, bq, bgate, bk, bv], axis=1).astype(BF16)
    w_log = jnp.pad(alog_in, ((0, 0), (0, LANES - alog_in.shape[1]))).astype(BF16)

    tm = _pick(math.gcd(t, l), 512)
    cos, sin_signed = _rope_tables(l)
    p2, lg2 = _proj0(x2, norm_g.reshape(1, d), w_main, w_log, cos, sin_signed,
                     qn_g.reshape(1, B_HD).astype(F32), kn_g.reshape(1, B_HD).astype(F32), tm)
    p3 = p2.reshape(b, l, P0_COLS)
    lg3 = lg2.reshape(b, l, LANES)

    tb = _pick(l, 512)
    q3, k3, v3, gc3 = _prep(p3, lg3, conv_w.astype(F32), _lane_row(a_log), _lane_row(dt_bias), tb)
    o_f, o_b = _scan(q3, k3, v3, gc3, tb, 2 if b % 2 == 0 else 1)

    yb = _attn(p3, _pick(l, 256), _pick(l, 512))

    return _out0(o_f.reshape(t, -1), o_b.reshape(t, -1), p2, yb.reshape(t, B_W),
                 onorm_g.reshape(1, A_DV).astype(F32), w_out.astype(BF16), x2,
                 next_norm_g.reshape(1, d).astype(F32), tm)


def _gmlp_layer(x2, h2, w_in, ln_g, ln_b, w_s, b_s, w_out):
    t, d = x2.shape
    cw = w_out.shape[0]
    tm = _pick(t, 512)
    p2 = _proj1(h2, w_in.astype(BF16), tm)
    return _out1(p2, ln_g.reshape(1, cw).astype(F32), ln_b.reshape(1, cw).astype(F32),
                 w_s.astype(BF16), jnp.transpose(b_s).astype(F32), w_out.astype(BF16), x2, tm)


def kernel(x, norm0_g, w_in0, conv0_w, a_log0, dt_bias0, a_onorm_g0, b_qnorm_g0, b_knorm_g0, w_out0,
           norm1_g, w_in1, c_ln_g1, c_ln_b1, c_ws1, c_bs1, w_out1):
    b, l, d = x.shape
    assert w_in0.shape[0] == 1 and w_in1.shape[0] == 1
    x2 = x.reshape(b * l, d)
    x2, h2 = _hybrid_layer(x2, b, l, norm0_g[0], w_in0[0], conv0_w[0], a_log0[0], dt_bias0[0],
                           a_onorm_g0[0], b_qnorm_g0[0], b_knorm_g0[0], w_out0[0], norm1_g[0])
    x2 = _gmlp_layer(x2, h2, w_in1[0], c_ln_g1[0], c_ln_b1[0], c_ws1[0], c_bs1[0], w_out1[0])
    return x2.reshape(b, l, d)
```

```python
import functools
import math

import jax
import jax.numpy as jnp
from jax import lax
from jax.experimental import pallas as pl
from jax.experimental.pallas import tpu as pltpu

F32 = jnp.float32
BF16 = jnp.bfloat16

EPS = 1e-6
GRID_W = 64
ROPE_THETA = 10000.0
A_HEADS = 8
A_DK = 128
A_DV = 128
A_CHUNK = 64
CONV_W = 5
A_W = A_HEADS * A_DK
B_HEADS = 8
B_KV_HEADS = 2
B_HD = 128
B_GROUP = B_HEADS // B_KV_HEADS
B_W = B_HEADS * B_HD
B_KV_W = B_KV_HEADS * B_HD
C_GROUPS = 16
C_CHUNK = 128

LANES = 128
VMEM_LIMIT = 56 * 1024 * 1024

COL_AQ = 0
COL_AK = A_W
COL_AV = 2 * A_W
COL_AGATE = 3 * A_W
COL_BQ = 4 * A_W
COL_BGATE = COL_BQ + B_W
COL_BK = COL_BGATE + B_W
COL_BV = COL_BK + B_KV_W
P0_COLS = COL_BV + B_KV_W


def _params(sem):
    return pltpu.CompilerParams(dimension_semantics=sem, vmem_limit_bytes=VMEM_LIMIT)


def _dot(a, b):
    return jnp.dot(a, b, preferred_element_type=F32)


def _dot_nt(a, b):
    return lax.dot_general(a, b, (((1,), (1,)), ((), ())), preferred_element_type=F32)


SUB_ROWS = 256
COL_CHUNK = 512
NORM_ROWS = 64


def _resident(shape):
    nd = len(shape)
    return pl.BlockSpec(shape, lambda *_: (0,) * nd, pipeline_mode=pl.Buffered(1))


def _rmsnorm_rows(x_ref, g_ref, h_ref, row0, nrows):
    for r in range(0, nrows, NORM_ROWS):
        x = x_ref[row0 + r:row0 + r + NORM_ROWS, :]
        ms = jnp.mean(x * x, axis=-1, keepdims=True)
        h_ref[r:r + NORM_ROWS, :] = (x * lax.rsqrt(ms + EPS) * g_ref[...]).astype(BF16)


def _silu(x):
    return x * jax.nn.sigmoid(x)


def _norm_rope(x, g, cos, sin_signed):
    y = x * lax.rsqrt(jnp.mean(x * x, axis=-1, keepdims=True) + EPS) * g
    lane = lax.broadcasted_iota(jnp.int32, y.shape, 1)
    quarter = B_HD // 4
    partner = jnp.where(lane % (2 * quarter) < quarter,
                        pltpu.roll(y, B_HD - quarter, axis=1), pltpu.roll(y, quarter, axis=1))
    return y * cos + partner * sin_signed


Q_SCALE = (B_HD ** -0.5) * math.log2(math.e)


def _proj0_epilogue(y, c0, cos, sin_signed, qg, kg):
    if COL_AGATE <= c0 < COL_BQ or COL_BGATE <= c0 < COL_BK:
        return _silu(y)
    if COL_BQ <= c0 < COL_BGATE or COL_BK <= c0 < COL_BV:
        heads = []
        for h0 in range(0, COL_CHUNK, B_HD):
            yh = y[:, h0:h0 + B_HD]
            if COL_BQ <= c0 + h0 < COL_BGATE:
                yh = _norm_rope(yh, qg, cos, sin_signed) * Q_SCALE
            elif COL_BK <= c0 + h0 < COL_BV:
                yh = _norm_rope(yh, kg, cos, sin_signed)
            heads.append(yh)
        return jnp.concatenate(heads, axis=1)
    return y


def _proj0_kernel(x_ref, g_ref, w_ref, wl_ref, cos_ref, sin_ref, qg_ref, kg_ref, p_ref, lg_ref, h_ref):
    n = w_ref.shape[1]
    for sub in range(x_ref.shape[0] // SUB_ROWS):
        h_buf = h_ref.at[sub % 2]
        _rmsnorm_rows(x_ref, g_ref, h_buf, sub * SUB_ROWS, SUB_ROWS)
        rows = slice(sub * SUB_ROWS, (sub + 1) * SUB_ROWS)
        lg_ref[rows, :] = _dot(h_buf[...], wl_ref[...])
        for c0 in range(0, n, COL_CHUNK):
            y = _dot(h_buf[...], w_ref[:, c0:c0 + COL_CHUNK])
            y = _proj0_epilogue(y, c0, cos_ref[rows, :], sin_ref[rows, :], qg_ref[...], kg_ref[...])
            p_ref[rows, c0:c0 + COL_CHUNK] = y.astype(BF16)


def _proj0(x2, g, w_main, w_log, cos, sin_signed, qg, kg, tm):
    t, d = x2.shape
    n = w_main.shape[1]
    l = cos.shape[0]
    assert l % tm == 0
    return pl.pallas_call(
        _proj0_kernel,
        grid=(t // tm,),
        in_specs=[
            pl.BlockSpec((tm, d), lambda i: (i, 0)),
            _resident((1, d)),
            _resident((d, n)),
            _resident((d, LANES)),
            pl.BlockSpec((tm, B_HD), lambda i: (i % (l // tm), 0)),
            pl.BlockSpec((tm, B_HD), lambda i: (i % (l // tm), 0)),
            _resident((1, B_HD)),
            _resident((1, B_HD)),
        ],
        out_specs=[
            pl.BlockSpec((tm, n), lambda i: (i, 0)),
            pl.BlockSpec((tm, LANES), lambda i: (i, 0)),
        ],
        out_shape=[
            jax.ShapeDtypeStruct((t, n), BF16),
            jax.ShapeDtypeStruct((t, LANES), F32),
        ],
        scratch_shapes=[pltpu.VMEM((2, SUB_ROWS, d), BF16)],
        compiler_params=_params(("arbitrary",)),
    )(x2, g, w_main, w_log, cos, sin_signed, qg, kg)


HALO = 16


def _softplus(x):
    return jnp.maximum(x, 0.0) + jnp.log1p(jnp.exp(-jnp.abs(x)))


def _prep_kernel(main_ref, hb_ref, ha_ref, cw_ref, lg_ref, alog_ref, dtb_ref,
                 q_ref, k_ref, v_ref, gc_ref):
    i = pl.program_id(1)
    first = i == 0
    last = i == pl.num_programs(1) - 1
    tb = main_ref.shape[1]
    n_strips = 3 * A_W // LANES
    for s in range(n_strips):
        cs = slice(s * LANES, (s + 1) * LANES)
        xm = main_ref[0, :, cs].astype(F32)
        xb = jnp.where(first, 0.0, hb_ref[0, :, cs].astype(F32))
        xa = jnp.where(last, 0.0, ha_ref[0, :, cs].astype(F32))
        xc = jnp.concatenate([xb, xm, xa], axis=0)
        acc = None
        for j in range(CONV_W):
            shift = (CONV_W // 2 - j) % (tb + 2 * HALO)
            xs = xc if shift == 0 else pltpu.roll(xc, shift, axis=0)
            term = xs[HALO:HALO + tb] * cw_ref[j:j + 1, cs]
            acc = term if acc is None else acc + term
        y = acc * jax.nn.sigmoid(acc)
        head = s % A_HEADS
        hs = slice(head * LANES, (head + 1) * LANES)
        if s < A_HEADS:
            y = y * lax.rsqrt(jnp.sum(y * y, axis=-1, keepdims=True) + EPS) * (A_DK ** -0.5)
            q_ref[0, :, hs] = y.astype(BF16)
        elif s < 2 * A_HEADS:
            y = y * lax.rsqrt(jnp.sum(y * y, axis=-1, keepdims=True) + EPS)
            k_ref[0, :, hs] = y.astype(BF16)
        else:
            v_ref[0, :, hs] = y.astype(BF16)

    lg = lg_ref[0]
    g = -jnp.exp(alog_ref[...]) * _softplus(lg + dtb_ref[...])
    beta = jax.nn.sigmoid(lg)
    row = lax.broadcasted_iota(jnp.int32, (tb, LANES), 0) % A_CHUNK
    lane = lax.broadcasted_iota(jnp.int32, (tb, LANES), 1)
    pf = g
    sf = g
    step = 1
    while step < A_CHUNK:
        pf = pf + jnp.where(row >= step, pltpu.roll(pf, step, axis=0), 0.0)
        sf = sf + jnp.where(row < A_CHUNK - step, pltpu.roll(sf, tb - step, axis=0), 0.0)
        step *= 2
    gc_ref[0] = jnp.where(lane < A_HEADS, pf, jnp.where(lane < 2 * A_HEADS, sf, beta))


def _prep(p3, lg3, conv_w, alog_l, dtb_l, tb):
    b, l, _ = p3.shape
    nb = l // tb
    hpb = tb // HALO
    qkv_shape = jax.ShapeDtypeStruct((b, l, A_W), BF16)
    return pl.pallas_call(
        _prep_kernel,
        grid=(b, nb),
        in_specs=[
            pl.BlockSpec((1, tb, 3 * A_W), lambda bi, i: (bi, i, 0)),
            pl.BlockSpec((1, HALO, 3 * A_W), lambda bi, i: (bi, jnp.maximum(i * hpb - 1, 0), 0)),
            pl.BlockSpec((1, HALO, 3 * A_W),
                         lambda bi, i: (bi, jnp.minimum((i + 1) * hpb, l // HALO - 1), 0)),
            pl.BlockSpec((CONV_W, 3 * A_W), lambda bi, i: (0, 0)),
            pl.BlockSpec((1, tb, LANES), lambda bi, i: (bi, i, 0)),
            pl.BlockSpec((1, LANES), lambda bi, i: (0, 0)),
            pl.BlockSpec((1, LANES), lambda bi, i: (0, 0)),
        ],
        out_specs=[
            pl.BlockSpec((1, tb, A_W), lambda bi, i: (bi, i, 0)),
            pl.BlockSpec((1, tb, A_W), lambda bi, i: (bi, i, 0)),
            pl.BlockSpec((1, tb, A_W), lambda bi, i: (bi, i, 0)),
            pl.BlockSpec((1, tb, LANES), lambda bi, i: (bi, i, 0)),
        ],
        out_shape=[qkv_shape, qkv_shape, qkv_shape, jax.ShapeDtypeStruct((b, l, LANES), F32)],
        compiler_params=_params(("arbitrary", "arbitrary")),
    )(p3, p3, p3, conv_w, lg3, alog_l, dtb_l)


def _delta_intra(chains):
    c = A_CHUNK
    n = len(chains)
    ri = lax.broadcasted_iota(jnp.int32, (c, c), 0)
    ci = lax.broadcasted_iota(jnp.int32, (c, c), 1)
    decay, eg, kb, k_bf, glast, strict = [], [], [], [], [], []
    vb, g_bs = [], []
    for q, k, v, gcol, bcol, grow, backward in chains:
        incl = ri <= ci if backward else ri >= ci
        strict.append(ri < ci if backward else ri > ci)
        g_bs.append(gcol)
        glast.append(gcol[0:1, :] if backward else gcol[c - 1:c, :])
        decay.append(jnp.where(incl, jnp.exp(gcol - grow), 0.0))
        eg.append(jnp.exp(gcol))
        kb.append(k * bcol)
        vb.append(v * bcol)
        k_bf.append(k.astype(BF16))
    kk = [_dot_nt(kb[i].astype(BF16), k_bf[i]) for i in range(n)]
    qk = [_dot_nt(chains[i][0].astype(BF16), k_bf[i]) for i in range(n)]
    nm = [jnp.where(strict[i], -kk[i] * decay[i], 0.0) for i in range(n)]
    eye = (ri == ci).astype(F32)
    t_inv = [eye + m for m in nm]
    step = 2
    while step < c:
        nm_bf = [m.astype(BF16) for m in nm]
        nm = [_dot(m, m) for m in nm_bf]
        upd = [_dot(nm[i].astype(BF16), t_inv[i].astype(BF16)) for i in range(n)]
        t_inv = [t_inv[i] + upd[i] for i in range(n)]
        step *= 2
    rhs = [_dot(t_inv[i].astype(BF16),
                jnp.concatenate([vb[i], kb[i] * eg[i]], axis=1).astype(BF16))
           for i in range(n)]
    return [(rhs[i][:, :A_DV], rhs[i][:, A_DV:].astype(BF16), (qk[i] * decay[i]).astype(BF16),
             (chains[i][0] * eg[i]).astype(BF16),
             (chains[i][1] * jnp.exp(glast[i] - g_bs[i])).T.astype(BF16), jnp.exp(glast[i]))
            for i in range(n)]


def _delta_inter(intra, s_refs):
    n = len(intra)
    s_old = [r[...] for r in s_refs]
    s_bf = [s.astype(BF16) for s in s_old]
    ws = [_dot(intra[i][1], s_bf[i]) for i in range(n)]
    qs = [_dot(intra[i][3], s_bf[i]) for i in range(n)]
    v_new_bf = [(intra[i][0] - ws[i]).astype(BF16) for i in range(n)]
    av = [_dot(intra[i][2], v_new_bf[i]) for i in range(n)]
    kv = [_dot(intra[i][4], v_new_bf[i]) for i in range(n)]
    for i in range(n):
        s_refs[i][...] = s_old[i] * intra[i][5] + kv[i]
    return [qs[i] + av[i] for i in range(n)]


SCAN_UNROLL = 1


def _scan_kernel(qf_ref, kf_ref, vf_ref, gf_ref, qb_ref, kb_ref, vb_ref, gb_ref,
                 of_ref, ob_ref, s_ref):
    @pl.when(pl.program_id(1) == 0)
    def _():
        s_ref[...] = jnp.zeros_like(s_ref)

    tb = qf_ref.shape[1]
    nc = tb // A_CHUNK

    def body(ci, carry):
        chains, s_refs, dests = [], [], []
        for u in range(SCAN_UNROLL):
            for bb in range(qf_ref.shape[0]):
                for backward, (q_ref, k_ref, v_ref, g_ref, o_ref) in enumerate(
                        ((qf_ref, kf_ref, vf_ref, gf_ref, of_ref), (qb_ref, kb_ref, vb_ref, gb_ref, ob_ref))):
                    cidx = ci * SCAN_UNROLL + u
                    cidx = nc - 1 - cidx if backward else cidx
                    rows = pl.ds(pl.multiple_of(cidx * A_CHUNK, A_CHUNK), A_CHUNK)
                    gtile = g_ref[bb, rows, :]
                    gt = gtile.T
                    for h in range(A_HEADS):
                        hs = slice(h * LANES, (h + 1) * LANES)
                        gl = backward * A_HEADS + h
                        bl = 2 * A_HEADS + gl
                        chains.append((q_ref[bb, rows, hs].astype(F32), k_ref[bb, rows, hs].astype(F32),
                                       v_ref[bb, rows, hs].astype(F32),
                                       gtile[:, gl:gl + 1], gtile[:, bl:bl + 1], gt[gl:gl + 1, :],
                                       bool(backward)))
                        s_refs.append(s_ref.at[bb * 2 * A_HEADS + gl])
                        dests.append((o_ref, bb, rows, hs))
        intra = _delta_intra(chains)
        per = len(chains) // SCAN_UNROLL
        for u in range(SCAN_UNROLL):
            sl = slice(u * per, (u + 1) * per)
            outs = _delta_inter(intra[sl], s_refs[sl])
            for (o_ref, bb, rows, hs), o in zip(dests[sl], outs):
                o_ref[bb, rows, hs] = o.astype(BF16)
        return carry

    lax.fori_loop(0, nc // SCAN_UNROLL, body, 0)


def _scan(q3, k3, v3, gc3, tb, bb):
    b, l, _ = q3.shape
    nb = l // tb
    fwd = lambda bi, i: (bi, i, 0)
    bwd = lambda bi, i: (bi, nb - 1 - i, 0)
    big = lambda m: pl.BlockSpec((bb, tb, A_W), m)
    small = lambda m: pl.BlockSpec((bb, tb, LANES), m)
    o_shape = jax.ShapeDtypeStruct((b, l, A_HEADS * A_DV), BF16)
    return pl.pallas_call(
        _scan_kernel,
        grid=(b // bb, nb),
        in_specs=[big(fwd), big(fwd), big(fwd), small(fwd), big(bwd), big(bwd), big(bwd), small(bwd)],
        out_specs=[big(fwd), big(bwd)],
        out_shape=[o_shape, o_shape],
        scratch_shapes=[pltpu.VMEM((bb * 2 * A_HEADS, A_DK, A_DV), F32)],
        compiler_params=_params(("arbitrary", "arbitrary")),
    )(q3, k3, v3, gc3, q3, k3, v3, gc3)


def _attn_kernel(q_ref, qn_ref, k_ref, v_ref, gate_ref, o_ref, vx_ref, s_ref, m_ref, acc_ref, *, tk):
    l = k_ref.shape[1]
    tq = q_ref.shape[1]

    def kv_rows(t):
        if isinstance(t, int):
            return slice(t * tk, (t + 1) * tk)
        return pl.ds(pl.multiple_of(t * tk, tk), tk)

    def scores(src_ref, t, buf):
        k_blk = k_ref[0, kv_rows(t), :]
        for hh in range(B_GROUP):
            rows = slice(hh * tq, (hh + 1) * tq)
            s_ref[buf, rows, :] = _dot_nt(src_ref[0, :, hh * B_HD:(hh + 1) * B_HD], k_blk)

    @pl.when(pl.program_id(2) == 0)
    def _():
        def vbody(r, carry):
            sl = kv_rows(r)
            vx_ref[sl, :B_HD] = v_ref[0, sl, :]
            vx_ref[sl, B_HD:] = jnp.ones((tk, B_HD), BF16)
            return carry
        lax.fori_loop(0, l // tk, vbody, 0)
        scores(q_ref, 0, 0)

    m_ref[...] = jnp.full_like(m_ref, -jnp.inf)
    acc_ref[...] = jnp.zeros_like(acc_ref)

    def softmax_pv(t, buf):
        v_blk = vx_ref[kv_rows(t), :]
        for hh in range(B_GROUP):
            rows = slice(hh * tq, (hh + 1) * tq)
            s = s_ref[buf, rows, :]
            m_prev = m_ref[rows, :]
            m_new = jnp.maximum(m_prev, jnp.max(s, axis=-1, keepdims=True))
            alpha = jnp.exp2(m_prev - m_new)
            p = jnp.concatenate(
                [jnp.exp2(s[:, c * LANES:(c + 1) * LANES] - m_new) for c in range(tk // LANES)],
                axis=1).astype(BF16)
            acc_ref[rows, :] = jnp.concatenate([alpha, alpha], axis=1) * acc_ref[rows, :] + _dot(p, v_blk)
            m_ref[rows, :] = m_new

    n_kv = l // tk
    for t in range(n_kv - 1):
        scores(q_ref, t + 1, (t + 1) % 2)
        softmax_pv(t, t % 2)
    scores(qn_ref, 0, 0)
    softmax_pv(n_kv - 1, (n_kv - 1) % 2)

    for hh in range(B_GROUP):
        rows = slice(hh * tq, (hh + 1) * tq)
        o = acc_ref[rows, :B_HD] / acc_ref[rows, B_HD:]
        gate = gate_ref[0, :, hh * B_HD:(hh + 1) * B_HD].astype(F32)
        o_ref[0, :, hh * B_HD:(hh + 1) * B_HD] = (o * gate).astype(BF16)


def _attn(p3, tq, tk):
    b, l, _ = p3.shape
    gw = B_GROUP * B_HD
    nq = l // tq
    return pl.pallas_call(
        functools.partial(_attn_kernel, tk=tk),
        grid=(b, B_KV_HEADS, nq),
        in_specs=[
            pl.BlockSpec((1, tq, gw), lambda bi, kv, i: (bi, i, COL_BQ // gw + kv)),
            pl.BlockSpec((1, tq, gw), lambda bi, kv, i: (bi, jnp.minimum(i + 1, nq - 1), COL_BQ // gw + kv)),
            pl.BlockSpec((1, l, B_HD), lambda bi, kv, i: (bi, 0, COL_BK // B_HD + kv)),
            pl.BlockSpec((1, l, B_HD), lambda bi, kv, i: (bi, 0, COL_BV // B_HD + kv)),
            pl.BlockSpec((1, tq, gw), lambda bi, kv, i: (bi, i, COL_BGATE // gw + kv)),
        ],
        out_specs=pl.BlockSpec((1, tq, gw), lambda bi, kv, i: (bi, i, kv)),
        out_shape=jax.ShapeDtypeStruct((b, l, B_W), BF16),
        scratch_shapes=[
            pltpu.VMEM((l, 2 * B_HD), BF16),
            pltpu.VMEM((2, B_GROUP * tq, tk), F32),
            pltpu.VMEM((B_GROUP * tq, B_HD), F32),
            pltpu.VMEM((B_GROUP * tq, 2 * B_HD), F32),
        ],
        compiler_params=_params(("arbitrary", "arbitrary", "arbitrary")),
    )(p3, p3, p3, p3, p3)


def _next_head_spec(tm, width, col, n_rows):
    last = n_rows // SUB_ROWS - 1
    per = tm // SUB_ROWS
    return pl.BlockSpec((SUB_ROWS, width), lambda i: (jnp.minimum((i + 1) * per, last), col))


def _out0_prologue(of_ref, ob_ref, ag_ref, yb_ref, og_ref, rows, y_buf):
    aw = A_HEADS * A_DV
    for h in range(A_HEADS):
        hs = slice(h * A_DV, (h + 1) * A_DV)
        o = of_ref[rows, hs].astype(F32) + ob_ref[rows, hs].astype(F32)
        o = o * lax.rsqrt(jnp.mean(o * o, axis=-1, keepdims=True) + EPS) * og_ref[...]
        y_buf[:, hs] = (o * ag_ref[rows, hs].astype(F32)).astype(BF16)
    y_buf[:, aw:] = yb_ref[rows, :]


def _out0_kernel(of_ref, ob_ref, ag_ref, yb_ref, ofn_ref, obn_ref, agn_ref, ybn_ref, og_ref, w_ref, x_ref,
                 g1_ref, o_ref, h1_ref, y_ref):
    d = w_ref.shape[1]
    n_sub = x_ref.shape[0] // SUB_ROWS
    head = slice(0, SUB_ROWS)
    main = (of_ref, ob_ref, ag_ref, yb_ref, og_ref)

    @pl.when(pl.program_id(0) == 0)
    def _():
        _out0_prologue(*main, head, y_ref.at[0])

    for sub in range(n_sub):
        rows = slice(sub * SUB_ROWS, (sub + 1) * SUB_ROWS)
        y_buf = y_ref.at[sub % 2]
        for c0 in range(0, d, COL_CHUNK):
            cs = slice(c0, c0 + COL_CHUNK)
            o_ref[rows, cs] = x_ref[rows, cs] + _dot(y_buf[...], w_ref[:, cs])
        if sub + 1 < n_sub:
            _out0_prologue(*main, slice((sub + 1) * SUB_ROWS, (sub + 2) * SUB_ROWS), y_ref.at[(sub + 1) % 2])
        else:
            _out0_prologue(ofn_ref, obn_ref, agn_ref, ybn_ref, og_ref, head, y_ref.at[0])
        _rmsnorm_rows(o_ref, g1_ref, h1_ref.at[rows], sub * SUB_ROWS, SUB_ROWS)


def _out0(o_f, o_b, p2, yb, og, w_out, x2, g1, tm):
    t, d = x2.shape
    aw = A_HEADS * A_DV
    kdim = aw + B_W
    assert (tm // SUB_ROWS) % 2 == 0
    return pl.pallas_call(
        _out0_kernel,
        grid=(t // tm,),
        in_specs=[
            pl.BlockSpec((tm, aw), lambda i: (i, 0)),
            pl.BlockSpec((tm, aw), lambda i: (i, 0)),
            pl.BlockSpec((tm, aw), lambda i: (i, COL_AGATE // aw)),
            pl.BlockSpec((tm, B_W), lambda i: (i, 0)),
            _next_head_spec(tm, aw, 0, t),
            _next_head_spec(tm, aw, 0, t),
            _next_head_spec(tm, aw, COL_AGATE // aw, t),
            _next_head_spec(tm, B_W, 0, t),
            _resident((1, A_DV)),
            _resident((kdim, d)),
            pl.BlockSpec((tm, d), lambda i: (i, 0)),
            _resident((1, d)),
        ],
        out_specs=[pl.BlockSpec((tm, d), lambda i: (i, 0)), pl.BlockSpec((tm, d), lambda i: (i, 0))],
        out_shape=[jax.ShapeDtypeStruct((t, d), F32), jax.ShapeDtypeStruct((t, d), BF16)],
        scratch_shapes=[pltpu.VMEM((2, SUB_ROWS, kdim), BF16)],
        compiler_params=_params(("arbitrary",)),
    )(o_f, o_b, p2, yb, o_f, o_b, p2, yb, og, w_out, x2, g1)


def _gelu(x):
    return 0.5 * x * (1.0 + lax.erf(x * (2.0 ** -0.5)))


def _proj1_kernel(h_ref, w_ref, p_ref):
    n = w_ref.shape[1]
    for sub in range(h_ref.shape[0] // SUB_ROWS):
        rows = slice(sub * SUB_ROWS, (sub + 1) * SUB_ROWS)
        for c0 in range(0, n, COL_CHUNK):
            cs = slice(c0, c0 + COL_CHUNK)
            y = _dot(h_ref[rows, :], w_ref[:, cs])
            act = _gelu(y) if c0 < 2 * n // 3 else y * jax.nn.sigmoid(y)
            p_ref[rows, cs] = act.astype(BF16)


def _proj1(h2, w, tm):
    t, d = h2.shape
    n = w.shape[1]
    return pl.pallas_call(
        _proj1_kernel,
        grid=(t // tm,),
        in_specs=[pl.BlockSpec((tm, d), lambda i: (i, 0)), _resident((d, n))],
        out_specs=pl.BlockSpec((tm, n), lambda i: (i, 0)),
        out_shape=jax.ShapeDtypeStruct((t, n), BF16),
        compiler_params=_params(("arbitrary",)),
    )(h2, w)


def _out1_prologue(u_ref, v_ref, gt_ref, lng_ref, lnb_ref, ws_ref, bst_ref, row0, z_buf):
    gw = v_ref.shape[1] // C_GROUPS
    for c in range(SUB_ROWS // C_CHUNK):
        rows = slice(row0 + c * C_CHUNK, row0 + (c + 1) * C_CHUNK)
        zrows = slice(c * C_CHUNK, (c + 1) * C_CHUNK)
        v = v_ref[rows, :].astype(F32)
        mu = jnp.mean(v, axis=-1, keepdims=True)
        vc = v - mu
        var = jnp.mean(vc * vc, axis=-1, keepdims=True)
        vn = (vc * lax.rsqrt(var + EPS) * lng_ref[...] + lnb_ref[...]).astype(BF16)
        for g in range(C_GROUPS):
            gs = slice(g * gw, (g + 1) * gw)
            s = _dot(ws_ref[g], vn[:, gs]) + bst_ref[:, g:g + 1]
            z = u_ref[rows, gs].astype(F32) * s * gt_ref[rows, gs].astype(F32)
            z_buf[zrows, gs] = z.astype(BF16)


def _out1_kernel(u_ref, v_ref, gt_ref, un_ref, vn_ref, gtn_ref, lng_ref, lnb_ref, ws_ref, bst_ref, w_ref,
                 x_ref, o_ref, z_ref):
    d = w_ref.shape[1]
    n_sub = x_ref.shape[0] // SUB_ROWS
    params = (lng_ref, lnb_ref, ws_ref, bst_ref)

    @pl.when(pl.program_id(0) == 0)
    def _():
        _out1_prologue(u_ref, v_ref, gt_ref, *params, 0, z_ref.at[0])

    for sub in range(n_sub):
        rows = slice(sub * SUB_ROWS, (sub + 1) * SUB_ROWS)
        z_buf = z_ref.at[sub % 2]
        for c0 in range(0, d, COL_CHUNK):
            cs = slice(c0, c0 + COL_CHUNK)
            o_ref[rows, cs] = x_ref[rows, cs] + _dot(z_buf[...], w_ref[:, cs])
        if sub + 1 < n_sub:
            _out1_prologue(u_ref, v_ref, gt_ref, *params, (sub + 1) * SUB_ROWS, z_ref.at[(sub + 1) % 2])
        else:
            _out1_prologue(un_ref, vn_ref, gtn_ref, *params, 0, z_ref.at[0])


def _out1(p2, lng, lnb, ws, bst, w_out, x2, tm):
    t, d = x2.shape
    cw = w_out.shape[0]
    assert (tm // SUB_ROWS) % 2 == 0
    return pl.pallas_call(
        _out1_kernel,
        grid=(t // tm,),
        in_specs=[
            pl.BlockSpec((tm, cw), lambda i: (i, 0)),
            pl.BlockSpec((tm, cw), lambda i: (i, 1)),
            pl.BlockSpec((tm, cw), lambda i: (i, 2)),
            _next_head_spec(tm, cw, 0, t),
            _next_head_spec(tm, cw, 1, t),
            _next_head_spec(tm, cw, 2, t),
            _resident((1, cw)),
            _resident((1, cw)),
            _resident((C_GROUPS, C_CHUNK, C_CHUNK)),
            _resident((C_CHUNK, C_GROUPS)),
            _resident((cw, d)),
            pl.BlockSpec((tm, d), lambda i: (i, 0)),
        ],
        out_specs=pl.BlockSpec((tm, d), lambda i: (i, 0)),
        out_shape=jax.ShapeDtypeStruct((t, d), F32),
        scratch_shapes=[pltpu.VMEM((2, SUB_ROWS, cw), BF16)],
        compiler_params=_params(("arbitrary",)),
    )(p2, p2, p2, p2, p2, p2, lng, lnb, ws, bst, w_out, x2)


def _rope_tables(l):
    t = jnp.arange(l)
    row = (t // GRID_W).astype(F32)
    col = (t % GRID_W).astype(F32)
    n_freq = B_HD // 4
    inv = ROPE_THETA ** (-jnp.arange(n_freq, dtype=F32) / n_freq)
    ar = row[:, None] * inv
    ac = col[:, None] * inv
    cos = jnp.concatenate([jnp.cos(ar), jnp.cos(ar), jnp.cos(ac), jnp.cos(ac)], axis=-1)
    sin = jnp.concatenate([-jnp.sin(ar), jnp.sin(ar), -jnp.sin(ac), jnp.sin(ac)], axis=-1)
    return cos, sin


def _lane_row(v):
    v = v.reshape(1, -1).astype(F32)
    return jnp.pad(v, ((0, 0), (0, LANES - v.shape[1])))


def _pick(n, pref):
    while n % pref:
        pref //= 2
    return pref


def _hybrid_layer(x2, b, l, norm_g, w_in, conv_w, a_log, dt_bias, onorm_g, qn_g, kn_g, w_out, next_norm_g):
    t, d = x2.shape
    sizes = (A_W, A_W, A_HEADS * A_DV, A_HEADS * A_DV, 4 * A_HEADS, B_W, B_KV_W, B_KV_W, B_W)
    offs = [0]
    for s in sizes:
        offs.append(offs[-1] + s)
    aq, ak, av, agate, alog_in, bq, bk, bv, bgate = [w_in[:, offs[i]:offs[i + 1]] for i in range(9)]
    w_main = jnp.concatenate([aq, ak, av, agate, bq, bgate, bk, bv], axis=1).astype(BF16)
    w_log = jnp.pad(alog_in, ((0, 0), (0, LANES - alog_in.shape[1]))).astype(BF16)

    tm = _pick(math.gcd(t, l), 512)
    cos, sin_signed = _rope_tables(l)
    p2, lg2 = _proj0(x2, norm_g.reshape(1, d), w_main, w_log, cos, sin_signed,
                     qn_g.reshape(1, B_HD).astype(F32), kn_g.reshape(1, B_HD).astype(F32), tm)
    p3 = p2.reshape(b, l, P0_COLS)
    lg3 = lg2.reshape(b, l, LANES)

    tb = _pick(l, 512)
    q3, k3, v3, gc3 = _prep(p3, lg3, conv_w.astype(F32), _lane_row(a_log), _lane_row(dt_bias), tb)
    o_f, o_b = _scan(q3, k3, v3, gc3, tb, 2 if b % 2 == 0 else 1)

    yb = _attn(p3, _pick(l, 256), _pick(l, 512))

    return _out0(o_f.reshape(t, -1), o_b.reshape(t, -1), p2, yb.reshape(t, B_W),
                 onorm_g.reshape(1, A_DV).astype(F32), w_out.astype(BF16), x2,
                 next_norm_g.reshape(1, d).astype(F32), tm)


def _gmlp_layer(x2, h2, w_in, ln_g, ln_b, w_s, b_s, w_out):
    t, d = x2.shape
    cw = w_out.shape[0]
    tm = _pick(t, 512)
    p2 = _proj1(h2, w_in.astype(BF16), tm)
    return _out1(p2, ln_g.reshape(1, cw).astype(F32), ln_b.reshape(1, cw).astype(F32),
                 w_s.astype(BF16), jnp.transpose(b_s).astype(F32), w_out.astype(BF16), x2, tm)


def kernel(x, norm0_g, w_in0, conv0_w, a_log0, dt_bias0, a_onorm_g0, b_qnorm_g0, b_knorm_g0, w_out0,
           norm1_g, w_in1, c_ln_g1, c_ln_b1, c_ws1, c_bs1, w_out1):
    b, l, d = x.shape
    assert w_in0.shape[0] == 1 and w_in1.shape[0] == 1
    x2 = x.reshape(b * l, d)
    x2, h2 = _hybrid_layer(x2, b, l, norm0_g[0], w_in0[0], conv0_w[0], a_log0[0], dt_bias0[0],
                           a_onorm_g0[0], b_qnorm_g0[0], b_knorm_g0[0], w_out0[0], norm1_g[0])
    x2 = _gmlp_layer(x2, h2, w_in1[0], c_ln_g1[0], c_ln_b1[0], c_ws1[0], c_bs1[0], w_out1[0])
    return x2.reshape(b, l, d)
```

```python
import functools
import math

import jax
import jax.numpy as jnp
from jax import lax
from jax.experimental import pallas as pl
from jax.experimental.pallas import tpu as pltpu

F32 = jnp.float32
BF16 = jnp.bfloat16

EPS = 1e-6
GRID_W = 64
ROPE_THETA = 10000.0
A_HEADS = 8
A_DK = 128
A_DV = 128
A_CHUNK = 64
CONV_W = 5
A_W = A_HEADS * A_DK
B_HEADS = 8
B_KV_HEADS = 2
B_HD = 128
B_GROUP = B_HEADS // B_KV_HEADS
B_W = B_HEADS * B_HD
B_KV_W = B_KV_HEADS * B_HD
C_GROUPS = 16
C_CHUNK = 128

LANES = 128
VMEM_LIMIT = 56 * 1024 * 1024

COL_AQ = 0
COL_AK = A_W
COL_AV = 2 * A_W
COL_AGATE = 3 * A_W
COL_BQ = 4 * A_W
COL_BGATE = COL_BQ + B_W
COL_BK = COL_BGATE + B_W
COL_BV = COL_BK + B_KV_W
P0_COLS = COL_BV + B_KV_W


def _params(sem):
    return pltpu.CompilerParams(dimension_semantics=sem, vmem_limit_bytes=VMEM_LIMIT)


def _dot(a, b):
    return jnp.dot(a, b, preferred_element_type=F32)


def _dot_nt(a, b):
    return lax.dot_general(a, b, (((1,), (1,)), ((), ())), preferred_element_type=F32)


SUB_ROWS = 256
COL_CHUNK = 512
NORM_ROWS = 64


def _resident(shape):
    nd = len(shape)
    return pl.BlockSpec(shape, lambda *_: (0,) * nd, pipeline_mode=pl.Buffered(1))


def _rmsnorm_rows(x_ref, g_ref, h_ref, row0, nrows):
    for r in range(0, nrows, NORM_ROWS):
        x = x_ref[row0 + r:row0 + r + NORM_ROWS, :]
        ms = jnp.mean(x * x, axis=-1, keepdims=True)
        h_ref[r:r + NORM_ROWS, :] = (x * lax.rsqrt(ms + EPS) * g_ref[...]).astype(BF16)


def _silu(x):
    return x * jax.nn.sigmoid(x)


def _norm_rope(x, g, cos, sin_signed):
    y = x * lax.rsqrt(jnp.mean(x * x, axis=-1, keepdims=True) + EPS) * g
    lane = lax.broadcasted_iota(jnp.int32, y.shape, 1)
    quarter = B_HD // 4
    partner = jnp.where(lane % (2 * quarter) < quarter,
                        pltpu.roll(y, B_HD - quarter, axis=1), pltpu.roll(y, quarter, axis=1))
    return y * cos + partner * sin_signed


Q_SCALE = (B_HD ** -0.5) * math.log2(math.e)


def _proj0_epilogue(y, c0, cos, sin_signed, qg, kg):
    if COL_AGATE <= c0 < COL_BQ or COL_BGATE <= c0 < COL_BK:
        return _silu(y)
    if COL_BQ <= c0 < COL_BGATE or COL_BK <= c0 < COL_BV:
        heads = []
        for h0 in range(0, COL_CHUNK, B_HD):
            yh = y[:, h0:h0 + B_HD]
            if COL_BQ <= c0 + h0 < COL_BGATE:
                yh = _norm_rope(yh, qg, cos, sin_signed) * Q_SCALE
            elif COL_BK <= c0 + h0 < COL_BV:
                yh = _norm_rope(yh, kg, cos, sin_signed)
            heads.append(yh)
        return jnp.concatenate(heads, axis=1)
    return y


def _proj0_kernel(x_ref, g_ref, w_ref, wl_ref, cos_ref, sin_ref, qg_ref, kg_ref, p_ref, lg_ref, h_ref):
    n = w_ref.shape[1]
    for sub in range(x_ref.shape[0] // SUB_ROWS):
        h_buf = h_ref.at[sub % 2]
        _rmsnorm_rows(x_ref, g_ref, h_buf, sub * SUB_ROWS, SUB_ROWS)
        rows = slice(sub * SUB_ROWS, (sub + 1) * SUB_ROWS)
        lg_ref[rows, :] = _dot(h_buf[...], wl_ref[...])
        for c0 in range(0, n, COL_CHUNK):
            y = _dot(h_buf[...], w_ref[:, c0:c0 + COL_CHUNK])
            y = _proj0_epilogue(y, c0, cos_ref[rows, :], sin_ref[rows, :], qg_ref[...], kg_ref[...])
            p_ref[rows, c0:c0 + COL_CHUNK] = y.astype(BF16)


def _proj0(x2, g, w_main, w_log, cos, sin_signed, qg, kg, tm):
    t, d = x2.shape
    n = w_main.shape[1]
    l = cos.shape[0]
    assert l % tm == 0
    return pl.pallas_call(
        _proj0_kernel,
        grid=(t // tm,),
        in_specs=[
            pl.BlockSpec((tm, d), lambda i: (i, 0)),
            _resident((1, d)),
            _resident((d, n)),
            _resident((d, LANES)),
            pl.BlockSpec((tm, B_HD), lambda i: (i % (l // tm), 0)),
            pl.BlockSpec((tm, B_HD), lambda i: (i % (l // tm), 0)),
            _resident((1, B_HD)),
            _resident((1, B_HD)),
        ],
        out_specs=[
            pl.BlockSpec((tm, n), lambda i: (i, 0)),
            pl.BlockSpec((tm, LANES), lambda i: (i, 0)),
        ],
        out_shape=[
            jax.ShapeDtypeStruct((t, n), BF16),
            jax.ShapeDtypeStruct((t, LANES), F32),
        ],
        scratch_shapes=[pltpu.VMEM((2, SUB_ROWS, d), BF16)],
        compiler_params=_params(("arbitrary",)),
    )(x2, g, w_main, w_log, cos, sin_signed, qg, kg)


HALO = 16
CONV_ROWS = 128


def _softplus(x):
    return jnp.maximum(x, 0.0) + jnp.log1p(jnp.exp(-jnp.abs(x)))


def _prep_kernel(main_ref, hb_ref, ha_ref, cw_ref, lg_ref, alog_ref, dtb_ref,
                 q_ref, k_ref, v_ref, gc_ref):
    i = pl.program_id(1)
    first = i == 0
    last = i == pl.num_programs(1) - 1
    tb = main_ref.shape[1]
    taps = [j for j in range(CONV_W) if j != CONV_W // 2]
    kwin = CONV_ROWS + 2 * HALO
    r = lax.broadcasted_iota(jnp.int32, (len(taps) * CONV_ROWS, kwin), 0)
    col = lax.broadcasted_iota(jnp.int32, (len(taps) * CONV_ROWS, kwin), 1)
    tap_idx = r // CONV_ROWS
    tap = tap_idx + (tap_idx >= CONV_W // 2).astype(jnp.int32)
    select = (col == r % CONV_ROWS + tap + (HALO - CONV_W // 2)).astype(BF16)
    for rb in range(tb // CONV_ROWS):
        r0 = rb * CONV_ROWS
        rows = slice(r0, r0 + CONV_ROWS)
        for n0 in range(0, 3 * A_W, COL_CHUNK):
            ns = slice(n0, n0 + COL_CHUNK)
            if rb == 0:
                before = jnp.where(first, jnp.zeros((), BF16), hb_ref[0, :, ns])
            else:
                before = main_ref[0, r0 - HALO:r0, ns]
            if r0 + CONV_ROWS == tb:
                after = jnp.where(last, jnp.zeros((), BF16), ha_ref[0, :, ns])
            else:
                after = main_ref[0, r0 + CONV_ROWS:r0 + CONV_ROWS + HALO, ns]
            mid = main_ref[0, rows, ns]
            shifted = _dot(select, jnp.concatenate([before, mid, after], axis=0))
            for l0 in range(0, COL_CHUNK, LANES):
                s = (n0 + l0) // LANES
                cs = slice(n0 + l0, n0 + l0 + LANES)
                acc = mid[:, l0:l0 + LANES].astype(F32) * cw_ref[CONV_W // 2:CONV_W // 2 + 1, cs]
                for ti, j in enumerate(taps):
                    acc = acc + shifted[ti * CONV_ROWS:(ti + 1) * CONV_ROWS, l0:l0 + LANES] * cw_ref[j:j + 1, cs]
                y = acc * jax.nn.sigmoid(acc)
                head = s % A_HEADS
                hs = slice(head * LANES, (head + 1) * LANES)
                if s < A_HEADS:
                    y = y * lax.rsqrt(jnp.sum(y * y, axis=-1, keepdims=True) + EPS) * (A_DK ** -0.5)
                    q_ref[0, rows, hs] = y.astype(BF16)
                elif s < 2 * A_HEADS:
                    y = y * lax.rsqrt(jnp.sum(y * y, axis=-1, keepdims=True) + EPS)
                    k_ref[0, rows, hs] = y.astype(BF16)
                else:
                    v_ref[0, rows, hs] = y.astype(BF16)

    lg = lg_ref[0]
    g = -jnp.exp(alog_ref[...]) * _softplus(lg + dtb_ref[...])
    beta = jax.nn.sigmoid(lg)
    row = lax.broadcasted_iota(jnp.int32, (tb, LANES), 0) % A_CHUNK
    lane = lax.broadcasted_iota(jnp.int32, (tb, LANES), 1)
    pf = g
    sf = g
    step = 1
    while step < A_CHUNK:
        pf = pf + jnp.where(row >= step, pltpu.roll(pf, step, axis=0), 0.0)
        sf = sf + jnp.where(row < A_CHUNK - step, pltpu.roll(sf, tb - step, axis=0), 0.0)
        step *= 2
    gc_ref[0] = jnp.where(lane < A_HEADS, pf, jnp.where(lane < 2 * A_HEADS, sf, beta))


def _prep(p3, lg3, conv_w, alog_l, dtb_l, tb):
    b, l, _ = p3.shape
    nb = l // tb
    hpb = tb // HALO
    qkv_shape = jax.ShapeDtypeStruct((b, l, A_W), BF16)
    return pl.pallas_call(
        _prep_kernel,
        grid=(b, nb),
        in_specs=[
            pl.BlockSpec((1, tb, 3 * A_W), lambda bi, i: (bi, i, 0)),
            pl.BlockSpec((1, HALO, 3 * A_W), lambda bi, i: (bi, jnp.maximum(i * hpb - 1, 0), 0)),
            pl.BlockSpec((1, HALO, 3 * A_W),
                         lambda bi, i: (bi, jnp.minimum((i + 1) * hpb, l // HALO - 1), 0)),
            pl.BlockSpec((CONV_W, 3 * A_W), lambda bi, i: (0, 0)),
            pl.BlockSpec((1, tb, LANES), lambda bi, i: (bi, i, 0)),
            pl.BlockSpec((1, LANES), lambda bi, i: (0, 0)),
            pl.BlockSpec((1, LANES), lambda bi, i: (0, 0)),
        ],
        out_specs=[
            pl.BlockSpec((1, tb, A_W), lambda bi, i: (bi, i, 0)),
            pl.BlockSpec((1, tb, A_W), lambda bi, i: (bi, i, 0)),
            pl.BlockSpec((1, tb, A_W), lambda bi, i: (bi, i, 0)),
            pl.BlockSpec((1, tb, LANES), lambda bi, i: (bi, i, 0)),
        ],
        out_shape=[qkv_shape, qkv_shape, qkv_shape, jax.ShapeDtypeStruct((b, l, LANES), F32)],
        compiler_params=_params(("arbitrary", "arbitrary")),
    )(p3, p3, p3, conv_w, lg3, alog_l, dtb_l)


def _delta_intra(chains):
    c = A_CHUNK
    n = len(chains)
    ri = lax.broadcasted_iota(jnp.int32, (c, c), 0)
    ci = lax.broadcasted_iota(jnp.int32, (c, c), 1)
    decay, eg, kb, k_bf, glast, strict = [], [], [], [], [], []
    vb, g_bs = [], []
    for q, k, v, gcol, bcol, grow, backward in chains:
        incl = ri <= ci if backward else ri >= ci
        strict.append(ri < ci if backward else ri > ci)
        g_bs.append(gcol)
        glast.append(gcol[0:1, :] if backward else gcol[c - 1:c, :])
        decay.append(jnp.where(incl, jnp.exp(gcol - grow), 0.0))
        eg.append(jnp.exp(gcol))
        kb.append(k * bcol)
        vb.append(v * bcol)
        k_bf.append(k.astype(BF16))
    kk = [_dot_nt(kb[i].astype(BF16), k_bf[i]) for i in range(n)]
    qk = [_dot_nt(chains[i][0].astype(BF16), k_bf[i]) for i in range(n)]
    nm = [jnp.where(strict[i], -kk[i] * decay[i], 0.0) for i in range(n)]
    eye = (ri == ci).astype(F32)
    t_inv = [eye + m for m in nm]
    step = 2
    while step < c:
        nm_bf = [m.astype(BF16) for m in nm]
        nm = [_dot(m, m) for m in nm_bf]
        upd = [_dot(nm[i].astype(BF16), t_inv[i].astype(BF16)) for i in range(n)]
        t_inv = [t_inv[i] + upd[i] for i in range(n)]
        step *= 2
    rhs = [_dot(t_inv[i].astype(BF16),
                jnp.concatenate([vb[i], kb[i] * eg[i]], axis=1).astype(BF16))
           for i in range(n)]
    return [(rhs[i][:, :A_DV], rhs[i][:, A_DV:].astype(BF16), (qk[i] * decay[i]).astype(BF16),
             (chains[i][0] * eg[i]).astype(BF16),
             (chains[i][1] * jnp.exp(glast[i] - g_bs[i])).T.astype(BF16), jnp.exp(glast[i]))
            for i in range(n)]


def _delta_inter(intra, s_refs):
    n = len(intra)
    s_old = [r[...] for r in s_refs]
    s_bf = [s.astype(BF16) for s in s_old]
    ws = [_dot(intra[i][1], s_bf[i]) for i in range(n)]
    qs = [_dot(intra[i][3], s_bf[i]) for i in range(n)]
    v_new_bf = [(intra[i][0] - ws[i]).astype(BF16) for i in range(n)]
    av = [_dot(intra[i][2], v_new_bf[i]) for i in range(n)]
    kv = [_dot(intra[i][4], v_new_bf[i]) for i in range(n)]
    for i in range(n):
        s_refs[i][...] = s_old[i] * intra[i][5] + kv[i]
    return [qs[i] + av[i] for i in range(n)]


def _scan_kernel(qf_ref, kf_ref, vf_ref, gf_ref, qb_ref, kb_ref, vb_ref, gb_ref,
                 of_ref, ob_ref, s_ref):
    @pl.when(pl.program_id(1) == 0)
    def _():
        s_ref[...] = jnp.zeros_like(s_ref)

    tb = qf_ref.shape[1]
    nc = tb // A_CHUNK

    def body(ci, carry):
        chains, s_refs, dests = [], [], []
        for bb in range(qf_ref.shape[0]):
            for backward, (q_ref, k_ref, v_ref, g_ref, o_ref) in enumerate(
                    ((qf_ref, kf_ref, vf_ref, gf_ref, of_ref), (qb_ref, kb_ref, vb_ref, gb_ref, ob_ref))):
                cidx = nc - 1 - ci if backward else ci
                rows = pl.ds(pl.multiple_of(cidx * A_CHUNK, A_CHUNK), A_CHUNK)
                gtile = g_ref[bb, rows, :]
                gt = gtile.T
                for h in range(A_HEADS):
                    hs = slice(h * LANES, (h + 1) * LANES)
                    gl = backward * A_HEADS + h
                    bl = 2 * A_HEADS + gl
                    chains.append((q_ref[bb, rows, hs].astype(F32), k_ref[bb, rows, hs].astype(F32),
                                   v_ref[bb, rows, hs].astype(F32),
                                   gtile[:, gl:gl + 1], gtile[:, bl:bl + 1], gt[gl:gl + 1, :],
                                   bool(backward)))
                    s_refs.append(s_ref.at[bb * 2 * A_HEADS + gl])
                    dests.append((o_ref, bb, rows, hs))
        outs = _delta_inter(_delta_intra(chains), s_refs)
        for (o_ref, bb, rows, hs), o in zip(dests, outs):
            o_ref[bb, rows, hs] = o.astype(BF16)
        return carry

    lax.fori_loop(0, nc, body, 0)


def _scan(q3, k3, v3, gc3, tb, bb):
    b, l, _ = q3.shape
    nb = l // tb
    fwd = lambda bi, i: (bi, i, 0)
    bwd = lambda bi, i: (bi, nb - 1 - i, 0)
    big = lambda m: pl.BlockSpec((bb, tb, A_W), m)
    small = lambda m: pl.BlockSpec((bb, tb, LANES), m)
    o_shape = jax.ShapeDtypeStruct((b, l, A_HEADS * A_DV), BF16)
    return pl.pallas_call(
        _scan_kernel,
        grid=(b // bb, nb),
        in_specs=[big(fwd), big(fwd), big(fwd), small(fwd), big(bwd), big(bwd), big(bwd), small(bwd)],
        out_specs=[big(fwd), big(bwd)],
        out_shape=[o_shape, o_shape],
        scratch_shapes=[pltpu.VMEM((bb * 2 * A_HEADS, A_DK, A_DV), F32)],
        compiler_params=_params(("arbitrary", "arbitrary")),
    )(q3, k3, v3, gc3, q3, k3, v3, gc3)


def _attn_kernel(q_ref, qn_ref, k_ref, v_ref, gate_ref, o_ref, vx_ref, s_ref, m_ref, acc_ref, *, tk):
    l = k_ref.shape[1]
    tq = q_ref.shape[1]

    def kv_rows(t):
        if isinstance(t, int):
            return slice(t * tk, (t + 1) * tk)
        return pl.ds(pl.multiple_of(t * tk, tk), tk)

    def scores(src_ref, t, buf):
        k_blk = k_ref[0, kv_rows(t), :]
        for hh in range(B_GROUP):
            rows = slice(hh * tq, (hh + 1) * tq)
            s_ref[buf, rows, :] = _dot_nt(src_ref[0, :, hh * B_HD:(hh + 1) * B_HD], k_blk)

    @pl.when(pl.program_id(2) == 0)
    def _():
        def vbody(r, carry):
            sl = kv_rows(r)
            vx_ref[sl, :B_HD] = v_ref[0, sl, :]
            vx_ref[sl, B_HD:] = jnp.ones((tk, B_HD), BF16)
            return carry
        lax.fori_loop(0, l // tk, vbody, 0)
        scores(q_ref, 0, 0)

    m_ref[...] = jnp.full_like(m_ref, -jnp.inf)
    acc_ref[...] = jnp.zeros_like(acc_ref)

    def softmax_pv(t, buf):
        v_blk = vx_ref[kv_rows(t), :]
        for hh in range(B_GROUP):
            rows = slice(hh * tq, (hh + 1) * tq)
            s = s_ref[buf, rows, :]
            m_prev = m_ref[rows, :]
            m_new = jnp.maximum(m_prev, jnp.max(s, axis=-1, keepdims=True))
            alpha = jnp.exp2(m_prev - m_new)
            p = jnp.concatenate(
                [jnp.exp2(s[:, c * LANES:(c + 1) * LANES] - m_new) for c in range(tk // LANES)],
                axis=1).astype(BF16)
            acc_ref[rows, :] = jnp.concatenate([alpha, alpha], axis=1) * acc_ref[rows, :] + _dot(p, v_blk)
            m_ref[rows, :] = m_new

    n_kv = l // tk
    for t in range(n_kv - 1):
        scores(q_ref, t + 1, (t + 1) % 2)
        softmax_pv(t, t % 2)
    scores(qn_ref, 0, 0)
    softmax_pv(n_kv - 1, (n_kv - 1) % 2)

    for hh in range(B_GROUP):
        rows = slice(hh * tq, (hh + 1) * tq)
        o = acc_ref[rows, :B_HD] / acc_ref[rows, B_HD:]
        gate = gate_ref[0, :, hh * B_HD:(hh + 1) * B_HD].astype(F32)
        o_ref[0, :, hh * B_HD:(hh + 1) * B_HD] = (o * gate).astype(BF16)


def _attn(p3, tq, tk):
    b, l, _ = p3.shape
    gw = B_GROUP * B_HD
    nq = l // tq
    return pl.pallas_call(
        functools.partial(_attn_kernel, tk=tk),
        grid=(b, B_KV_HEADS, nq),
        in_specs=[
            pl.BlockSpec((1, tq, gw), lambda bi, kv, i: (bi, i, COL_BQ // gw + kv)),
            pl.BlockSpec((1, tq, gw), lambda bi, kv, i: (bi, jnp.minimum(i + 1, nq - 1), COL_BQ // gw + kv)),
            pl.BlockSpec((1, l, B_HD), lambda bi, kv, i: (bi, 0, COL_BK // B_HD + kv)),
            pl.BlockSpec((1, l, B_HD), lambda bi, kv, i: (bi, 0, COL_BV // B_HD + kv)),
            pl.BlockSpec((1, tq, gw), lambda bi, kv, i: (bi, i, COL_BGATE // gw + kv)),
        ],
        out_specs=pl.BlockSpec((1, tq, gw), lambda bi, kv, i: (bi, i, kv)),
        out_shape=jax.ShapeDtypeStruct((b, l, B_W), BF16),
        scratch_shapes=[
            pltpu.VMEM((l, 2 * B_HD), BF16),
            pltpu.VMEM((2, B_GROUP * tq, tk), F32),
            pltpu.VMEM((B_GROUP * tq, B_HD), F32),
            pltpu.VMEM((B_GROUP * tq, 2 * B_HD), F32),
        ],
        compiler_params=_params(("arbitrary", "arbitrary", "arbitrary")),
    )(p3, p3, p3, p3, p3)


def _next_head_spec(tm, width, col, n_rows):
    last = n_rows // SUB_ROWS - 1
    per = tm // SUB_ROWS
    return pl.BlockSpec((SUB_ROWS, width), lambda i: (jnp.minimum((i + 1) * per, last), col))


def _out0_prologue(of_ref, ob_ref, ag_ref, yb_ref, og_ref, rows, y_buf):
    aw = A_HEADS * A_DV
    for h in range(A_HEADS):
        hs = slice(h * A_DV, (h + 1) * A_DV)
        o = of_ref[rows, hs].astype(F32) + ob_ref[rows, hs].astype(F32)
        o = o * lax.rsqrt(jnp.mean(o * o, axis=-1, keepdims=True) + EPS) * og_ref[...]
        y_buf[:, hs] = (o * ag_ref[rows, hs].astype(F32)).astype(BF16)
    y_buf[:, aw:] = yb_ref[rows, :]


def _out0_kernel(of_ref, ob_ref, ag_ref, yb_ref, og_ref, w_ref, x_ref, g1_ref, o_ref, h1_ref, y_ref):
    d = w_ref.shape[1]
    for sub in range(x_ref.shape[0] // SUB_ROWS):
        rows = slice(sub * SUB_ROWS, (sub + 1) * SUB_ROWS)
        y_buf = y_ref.at[sub % 2]
        _out0_prologue(of_ref, ob_ref, ag_ref, yb_ref, og_ref, rows, y_buf)
        for c0 in range(0, d, COL_CHUNK):
            cs = slice(c0, c0 + COL_CHUNK)
            o_ref[rows, cs] = x_ref[rows, cs] + _dot(y_buf[...], w_ref[:, cs])
        _rmsnorm_rows(o_ref, g1_ref, h1_ref.at[rows], sub * SUB_ROWS, SUB_ROWS)


def _out0(o_f, o_b, p2, yb, og, w_out, x2, g1, tm):
    t, d = x2.shape
    aw = A_HEADS * A_DV
    kdim = aw + B_W
    return pl.pallas_call(
        _out0_kernel,
        grid=(t // tm,),
        in_specs=[
            pl.BlockSpec((tm, aw), lambda i: (i, 0)),
            pl.BlockSpec((tm, aw), lambda i: (i, 0)),
            pl.BlockSpec((tm, aw), lambda i: (i, COL_AGATE // aw)),
            pl.BlockSpec((tm, B_W), lambda i: (i, 0)),
            _resident((1, A_DV)),
            _resident((kdim, d)),
            pl.BlockSpec((tm, d), lambda i: (i, 0)),
            _resident((1, d)),
        ],
        out_specs=[pl.BlockSpec((tm, d), lambda i: (i, 0)), pl.BlockSpec((tm, d), lambda i: (i, 0))],
        out_shape=[jax.ShapeDtypeStruct((t, d), F32), jax.ShapeDtypeStruct((t, d), BF16)],
        scratch_shapes=[pltpu.VMEM((2, SUB_ROWS, kdim), BF16)],
        compiler_params=_params(("arbitrary",)),
    )(o_f, o_b, p2, yb, og, w_out, x2, g1)


def _gelu(x):
    return 0.5 * x * (1.0 + lax.erf(x * (2.0 ** -0.5)))


def _proj1_kernel(h_ref, w_ref, p_ref):
    n = w_ref.shape[1]
    for sub in range(h_ref.shape[0] // SUB_ROWS):
        rows = slice(sub * SUB_ROWS, (sub + 1) * SUB_ROWS)
        for c0 in range(0, n, COL_CHUNK):
            cs = slice(c0, c0 + COL_CHUNK)
            y = _dot(h_ref[rows, :], w_ref[:, cs])
            act = _gelu(y) if c0 < 2 * n // 3 else y * jax.nn.sigmoid(y)
            p_ref[rows, cs] = act.astype(BF16)


def _proj1(h2, w, tm):
    t, d = h2.shape
    n = w.shape[1]
    return pl.pallas_call(
        _proj1_kernel,
        grid=(t // tm,),
        in_specs=[pl.BlockSpec((tm, d), lambda i: (i, 0)), _resident((d, n))],
        out_specs=pl.BlockSpec((tm, n), lambda i: (i, 0)),
        out_shape=jax.ShapeDtypeStruct((t, n), BF16),
        compiler_params=_params(("arbitrary",)),
    )(h2, w)


def _out1_prologue(u_ref, v_ref, gt_ref, lng_ref, lnb_ref, ws_ref, bst_ref, row0, z_buf):
    gw = v_ref.shape[1] // C_GROUPS
    for c in range(SUB_ROWS // C_CHUNK):
        rows = slice(row0 + c * C_CHUNK, row0 + (c + 1) * C_CHUNK)
        zrows = slice(c * C_CHUNK, (c + 1) * C_CHUNK)
        v = v_ref[rows, :].astype(F32)
        mu = jnp.mean(v, axis=-1, keepdims=True)
        vc = v - mu
        var = jnp.mean(vc * vc, axis=-1, keepdims=True)
        vn = (vc * lax.rsqrt(var + EPS) * lng_ref[...] + lnb_ref[...]).astype(BF16)
        for g in range(C_GROUPS):
            gs = slice(g * gw, (g + 1) * gw)
            s = _dot(ws_ref[g], vn[:, gs]) + bst_ref[:, g:g + 1]
            z = u_ref[rows, gs].astype(F32) * s * gt_ref[rows, gs].astype(F32)
            z_buf[zrows, gs] = z.astype(BF16)


def _out1_kernel(u_ref, v_ref, gt_ref, un_ref, vn_ref, gtn_ref, lng_ref, lnb_ref, ws_ref, bst_ref, w_ref,
                 x_ref, o_ref, z_ref):
    d = w_ref.shape[1]
    n_sub = x_ref.shape[0] // SUB_ROWS
    params = (lng_ref, lnb_ref, ws_ref, bst_ref)

    @pl.when(pl.program_id(0) == 0)
    def _():
        _out1_prologue(u_ref, v_ref, gt_ref, *params, 0, z_ref.at[0])

    for sub in range(n_sub):
        rows = slice(sub * SUB_ROWS, (sub + 1) * SUB_ROWS)
        z_buf = z_ref.at[sub % 2]
        for c0 in range(0, d, COL_CHUNK):
            cs = slice(c0, c0 + COL_CHUNK)
            o_ref[rows, cs] = x_ref[rows, cs] + _dot(z_buf[...], w_ref[:, cs])
        if sub + 1 < n_sub:
            _out1_prologue(u_ref, v_ref, gt_ref, *params, (sub + 1) * SUB_ROWS, z_ref.at[(sub + 1) % 2])
        else:
            _out1_prologue(un_ref, vn_ref, gtn_ref, *params, 0, z_ref.at[0])


def _out1(p2, lng, lnb, ws, bst, w_out, x2, tm):
    t, d = x2.shape
    cw = w_out.shape[0]
    assert (tm // SUB_ROWS) % 2 == 0
    return pl.pallas_call(
        _out1_kernel,
        grid=(t // tm,),
        in_specs=[
            pl.BlockSpec((tm, cw), lambda i: (i, 0)),
            pl.BlockSpec((tm, cw), lambda i: (i, 1)),
            pl.BlockSpec((tm, cw), lambda i: (i, 2)),
            _next_head_spec(tm, cw, 0, t),
            _next_head_spec(tm, cw, 1, t),
            _next_head_spec(tm, cw, 2, t),
            _resident((1, cw)),
            _resident((1, cw)),
            _resident((C_GROUPS, C_CHUNK, C_CHUNK)),
            _resident((C_CHUNK, C_GROUPS)),
            _resident((cw, d)),
            pl.BlockSpec((tm, d), lambda i: (i, 0)),
        ],
        out_specs=pl.BlockSpec((tm, d), lambda i: (i, 0)),
        out_shape=jax.ShapeDtypeStruct((t, d), F32),
        scratch_shapes=[pltpu.VMEM((2, SUB_ROWS, cw), BF16)],
        compiler_params=_params(("arbitrary",)),
    )(p2, p2, p2, p2, p2, p2, lng, lnb, ws, bst, w_out, x2)


def _rope_tables(l):
    t = jnp.arange(l)
    row = (t // GRID_W).astype(F32)
    col = (t % GRID_W).astype(F32)
    n_freq = B_HD // 4
    inv = ROPE_THETA ** (-jnp.arange(n_freq, dtype=F32) / n_freq)
    ar = row[:, None] * inv
    ac = col[:, None] * inv
    cos = jnp.concatenate([jnp.cos(ar), jnp.cos(ar), jnp.cos(ac), jnp.cos(ac)], axis=-1)
    sin = jnp.concatenate([-jnp.sin(ar), jnp.sin(ar), -jnp.sin(ac), jnp.sin(ac)], axis=-1)
    return cos, sin


def _lane_row(v):
    v = v.reshape(1, -1).astype(F32)
    return jnp.pad(v, ((0, 0), (0, LANES - v.shape[1])))


def _pick(n, pref):
    while n % pref:
        pref //= 2
    return pref


def _hybrid_layer(x2, b, l, norm_g, w_in, conv_w, a_log, dt_bias, onorm_g, qn_g, kn_g, w_out, next_norm_g):
    t, d = x2.shape
    sizes = (A_W, A_W, A_HEADS * A_DV, A_HEADS * A_DV, 4 * A_HEADS, B_W, B_KV_W, B_KV_W, B_W)
    offs = [0]
    for s in sizes:
        offs.append(offs[-1] + s)
    aq, ak, av, agate, alog_in, bq, bk, bv, bgate = [w_in[:, offs[i]:offs[i + 1]] for i in range(9)]
    w_main = jnp.concatenate([aq, ak, av, agate, bq, bgate, bk, bv], axis=1).astype(BF16)
    w_log = jnp.pad(alog_in, ((0, 0), (0, LANES - alog_in.shape[1]))).astype(BF16)

    tm = _pick(math.gcd(t, l), 512)
    cos, sin_signed = _rope_tables(l)
    p2, lg2 = _proj0(x2, norm_g.reshape(1, d), w_main, w_log, cos, sin_signed,
                     qn_g.reshape(1, B_HD).astype(F32), kn_g.reshape(1, B_HD).astype(F32), tm)
    p3 = p2.reshape(b, l, P0_COLS)
    lg3 = lg2.reshape(b, l, LANES)

    tb = _pick(l, 512)
    q3, k3, v3, gc3 = _prep(p3, lg3, conv_w.astype(F32), _lane_row(a_log), _lane_row(dt_bias), tb)
    o_f, o_b = _scan(q3, k3, v3, gc3, tb, 2 if b % 2 == 0 else 1)

    yb = _attn(p3, _pick(l, 256), _pick(l, 512))

    return _out0(o_f.reshape(t, -1), o_b.reshape(t, -1), p2, yb.reshape(t, B_W),
                 onorm_g.reshape(1, A_DV).astype(F32), w_out.astype(BF16), x2,
                 next_norm_g.reshape(1, d).astype(F32), tm)


def _gmlp_layer(x2, h2, w_in, ln_g, ln_b, w_s, b_s, w_out):
    t, d = x2.shape
    cw = w_out.shape[0]
    tm = _pick(t, 512)
    p2 = _proj1(h2, w_in.astype(BF16), tm)
    return _out1(p2, ln_g.reshape(1, cw).astype(F32), ln_b.reshape(1, cw).astype(F32),
                 w_s.astype(BF16), jnp.transpose(b_s).astype(F32), w_out.astype(BF16), x2, tm)


def kernel(x, norm0_g, w_in0, conv0_w, a_log0, dt_bias0, a_onorm_g0, b_qnorm_g0, b_knorm_g0, w_out0,
           norm1_g, w_in1, c_ln_g1, c_ln_b1, c_ws1, c_bs1, w_out1):
    b, l, d = x.shape
    assert w_in0.shape[0] == 1 and w_in1.shape[0] == 1
    x2 = x.reshape(b * l, d)
    x2, h2 = _hybrid_layer(x2, b, l, norm0_g[0], w_in0[0], conv0_w[0], a_log0[0], dt_bias0[0],
                           a_onorm_g0[0], b_qnorm_g0[0], b_knorm_g0[0], w_out0[0], norm1_g[0])
    x2 = _gmlp_layer(x2, h2, w_in1[0], c_ln_g1[0], c_ln_b1[0], c_ws1[0], c_bs1[0], w_out1[0])
    return x2.reshape(b, l, d)
```

```python
import functools
import math

import jax
import jax.numpy as jnp
from jax import lax
from jax.experimental import pallas as pl
from jax.experimental.pallas import tpu as pltpu

F32 = jnp.float32
BF16 = jnp.bfloat16

EPS = 1e-6
GRID_W = 64
ROPE_THETA = 10000.0
A_HEADS = 8
A_DK = 128
A_DV = 128
A_CHUNK = 64
CONV_W = 5
A_W = A_HEADS * A_DK
B_HEADS = 8
B_KV_HEADS = 2
B_HD = 128
B_GROUP = B_HEADS // B_KV_HEADS
B_W = B_HEADS * B_HD
B_KV_W = B_KV_HEADS * B_HD
C_GROUPS = 16
C_CHUNK = 128

LANES = 128
VMEM_LIMIT = 56 * 1024 * 1024

COL_AQ = 0
COL_AK = A_W
COL_AV = 2 * A_W
COL_AGATE = 3 * A_W
COL_BQ = 4 * A_W
COL_BGATE = COL_BQ + B_W
COL_BK = COL_BGATE + B_W
COL_BV = COL_BK + B_KV_W
P0_COLS = COL_BV + B_KV_W


def _params(sem):
    return pltpu.CompilerParams(dimension_semantics=sem, vmem_limit_bytes=VMEM_LIMIT)


def _dot(a, b):
    return jnp.dot(a, b, preferred_element_type=F32)


def _dot_nt(a, b):
    return lax.dot_general(a, b, (((1,), (1,)), ((), ())), preferred_element_type=F32)


SUB_ROWS = 256
COL_CHUNK = 512
NORM_ROWS = 64


def _resident(shape):
    nd = len(shape)
    return pl.BlockSpec(shape, lambda *_: (0,) * nd, pipeline_mode=pl.Buffered(1))


def _rmsnorm_rows(x_ref, g_ref, h_ref, row0, nrows):
    for r in range(0, nrows, NORM_ROWS):
        x = x_ref[row0 + r:row0 + r + NORM_ROWS, :]
        ms = jnp.mean(x * x, axis=-1, keepdims=True)
        h_ref[r:r + NORM_ROWS, :] = (x * lax.rsqrt(ms + EPS) * g_ref[...]).astype(BF16)


def _silu(x):
    return x * jax.nn.sigmoid(x)


def _norm_rope(x, g, cos, sin_signed):
    y = x * lax.rsqrt(jnp.mean(x * x, axis=-1, keepdims=True) + EPS) * g
    lane = lax.broadcasted_iota(jnp.int32, y.shape, 1)
    quarter = B_HD // 4
    partner = jnp.where(lane % (2 * quarter) < quarter,
                        pltpu.roll(y, B_HD - quarter, axis=1), pltpu.roll(y, quarter, axis=1))
    return y * cos + partner * sin_signed


Q_SCALE = (B_HD ** -0.5) * math.log2(math.e)


def _proj0_epilogue(y, c0, cos, sin_signed, qg, kg):
    if COL_AGATE <= c0 < COL_BQ or COL_BGATE <= c0 < COL_BK:
        return _silu(y)
    if COL_BQ <= c0 < COL_BGATE or COL_BK <= c0 < COL_BV:
        heads = []
        for h0 in range(0, COL_CHUNK, B_HD):
            yh = y[:, h0:h0 + B_HD]
            if COL_BQ <= c0 + h0 < COL_BGATE:
                yh = _norm_rope(yh, qg, cos, sin_signed) * Q_SCALE
            elif COL_BK <= c0 + h0 < COL_BV:
                yh = _norm_rope(yh, kg, cos, sin_signed)
            heads.append(yh)
        return jnp.concatenate(heads, axis=1)
    return y


def _proj0_kernel(x_ref, g_ref, w_ref, wl_ref, cos_ref, sin_ref, qg_ref, kg_ref, p_ref, lg_ref, h_ref):
    n = w_ref.shape[1]
    for sub in range(x_ref.shape[0] // SUB_ROWS):
        h_buf = h_ref.at[sub % 2]
        _rmsnorm_rows(x_ref, g_ref, h_buf, sub * SUB_ROWS, SUB_ROWS)
        rows = slice(sub * SUB_ROWS, (sub + 1) * SUB_ROWS)
        lg_ref[rows, :] = _dot(h_buf[...], wl_ref[...])
        for c0 in range(0, n, COL_CHUNK):
            y = _dot(h_buf[...], w_ref[:, c0:c0 + COL_CHUNK])
            y = _proj0_epilogue(y, c0, cos_ref[rows, :], sin_ref[rows, :], qg_ref[...], kg_ref[...])
            p_ref[rows, c0:c0 + COL_CHUNK] = y.astype(BF16)


def _proj0(x2, g, w_main, w_log, cos, sin_signed, qg, kg, tm):
    t, d = x2.shape
    n = w_main.shape[1]
    l = cos.shape[0]
    assert l % tm == 0
    return pl.pallas_call(
        _proj0_kernel,
        grid=(t // tm,),
        in_specs=[
            pl.BlockSpec((tm, d), lambda i: (i, 0)),
            _resident((1, d)),
            _resident((d, n)),
            _resident((d, LANES)),
            pl.BlockSpec((tm, B_HD), lambda i: (i % (l // tm), 0)),
            pl.BlockSpec((tm, B_HD), lambda i: (i % (l // tm), 0)),
            _resident((1, B_HD)),
            _resident((1, B_HD)),
        ],
        out_specs=[
            pl.BlockSpec((tm, n), lambda i: (i, 0)),
            pl.BlockSpec((tm, LANES), lambda i: (i, 0)),
        ],
        out_shape=[
            jax.ShapeDtypeStruct((t, n), BF16),
            jax.ShapeDtypeStruct((t, LANES), F32),
        ],
        scratch_shapes=[pltpu.VMEM((2, SUB_ROWS, d), BF16)],
        compiler_params=_params(("arbitrary",)),
    )(x2, g, w_main, w_log, cos, sin_signed, qg, kg)


HALO = 16
CONV_ROWS = 128


def _softplus(x):
    return jnp.maximum(x, 0.0) + jnp.log1p(jnp.exp(-jnp.abs(x)))


def _prep_kernel(main_ref, hb_ref, ha_ref, cw_ref, lg_ref, alog_ref, dtb_ref,
                 q_ref, k_ref, v_ref, gc_ref):
    i = pl.program_id(1)
    first = i == 0
    last = i == pl.num_programs(1) - 1
    tb = main_ref.shape[1]
    taps = [j for j in range(CONV_W) if j != CONV_W // 2]
    kwin = CONV_ROWS + 2 * HALO
    r = lax.broadcasted_iota(jnp.int32, (len(taps) * CONV_ROWS, kwin), 0)
    col = lax.broadcasted_iota(jnp.int32, (len(taps) * CONV_ROWS, kwin), 1)
    tap_idx = r // CONV_ROWS
    tap = tap_idx + (tap_idx >= CONV_W // 2).astype(jnp.int32)
    select = (col == r % CONV_ROWS + tap + (HALO - CONV_W // 2)).astype(BF16)
    for rb in range(tb // CONV_ROWS):
        r0 = rb * CONV_ROWS
        rows = slice(r0, r0 + CONV_ROWS)
        for n0 in range(0, 3 * A_W, COL_CHUNK):
            ns = slice(n0, n0 + COL_CHUNK)
            if rb == 0:
                before = jnp.where(first, jnp.zeros((), BF16), hb_ref[0, :, ns])
            else:
                before = main_ref[0, r0 - HALO:r0, ns]
            if r0 + CONV_ROWS == tb:
                after = jnp.where(last, jnp.zeros((), BF16), ha_ref[0, :, ns])
            else:
                after = main_ref[0, r0 + CONV_ROWS:r0 + CONV_ROWS + HALO, ns]
            mid = main_ref[0, rows, ns]
            shifted = _dot(select, jnp.concatenate([before, mid, after], axis=0))
            for l0 in range(0, COL_CHUNK, LANES):
                s = (n0 + l0) // LANES
                cs = slice(n0 + l0, n0 + l0 + LANES)
                acc = mid[:, l0:l0 + LANES].astype(F32) * cw_ref[CONV_W // 2:CONV_W // 2 + 1, cs]
                for ti, j in enumerate(taps):
                    acc = acc + shifted[ti * CONV_ROWS:(ti + 1) * CONV_ROWS, l0:l0 + LANES] * cw_ref[j:j + 1, cs]
                y = acc * jax.nn.sigmoid(acc)
                head = s % A_HEADS
                hs = slice(head * LANES, (head + 1) * LANES)
                if s < A_HEADS:
                    y = y * lax.rsqrt(jnp.sum(y * y, axis=-1, keepdims=True) + EPS) * (A_DK ** -0.5)
                    q_ref[0, rows, hs] = y.astype(BF16)
                elif s < 2 * A_HEADS:
                    y = y * lax.rsqrt(jnp.sum(y * y, axis=-1, keepdims=True) + EPS)
                    k_ref[0, rows, hs] = y.astype(BF16)
                else:
                    v_ref[0, rows, hs] = y.astype(BF16)

    lg = lg_ref[0]
    g = -jnp.exp(alog_ref[...]) * _softplus(lg + dtb_ref[...])
    beta = jax.nn.sigmoid(lg)
    row = lax.broadcasted_iota(jnp.int32, (tb, LANES), 0) % A_CHUNK
    lane = lax.broadcasted_iota(jnp.int32, (tb, LANES), 1)
    pf = g
    sf = g
    step = 1
    while step < A_CHUNK:
        pf = pf + jnp.where(row >= step, pltpu.roll(pf, step, axis=0), 0.0)
        sf = sf + jnp.where(row < A_CHUNK - step, pltpu.roll(sf, tb - step, axis=0), 0.0)
        step *= 2
    gc_ref[0] = jnp.where(lane < A_HEADS, pf, jnp.where(lane < 2 * A_HEADS, sf, beta))


def _prep(p3, lg3, conv_w, alog_l, dtb_l, tb):
    b, l, _ = p3.shape
    nb = l // tb
    hpb = tb // HALO
    qkv_shape = jax.ShapeDtypeStruct((b, l, A_W), BF16)
    return pl.pallas_call(
        _prep_kernel,
        grid=(b, nb),
        in_specs=[
            pl.BlockSpec((1, tb, 3 * A_W), lambda bi, i: (bi, i, 0)),
            pl.BlockSpec((1, HALO, 3 * A_W), lambda bi, i: (bi, jnp.maximum(i * hpb - 1, 0), 0)),
            pl.BlockSpec((1, HALO, 3 * A_W),
                         lambda bi, i: (bi, jnp.minimum((i + 1) * hpb, l // HALO - 1), 0)),
            pl.BlockSpec((CONV_W, 3 * A_W), lambda bi, i: (0, 0)),
            pl.BlockSpec((1, tb, LANES), lambda bi, i: (bi, i, 0)),
            pl.BlockSpec((1, LANES), lambda bi, i: (0, 0)),
            pl.BlockSpec((1, LANES), lambda bi, i: (0, 0)),
        ],
        out_specs=[
            pl.BlockSpec((1, tb, A_W), lambda bi, i: (bi, i, 0)),
            pl.BlockSpec((1, tb, A_W), lambda bi, i: (bi, i, 0)),
            pl.BlockSpec((1, tb, A_W), lambda bi, i: (bi, i, 0)),
            pl.BlockSpec((1, tb, LANES), lambda bi, i: (bi, i, 0)),
        ],
        out_shape=[qkv_shape, qkv_shape, qkv_shape, jax.ShapeDtypeStruct((b, l, LANES), F32)],
        compiler_params=_params(("arbitrary", "arbitrary")),
    )(p3, p3, p3, conv_w, lg3, alog_l, dtb_l)


def _delta_intra(chains):
    c = A_CHUNK
    n = len(chains)
    ri = lax.broadcasted_iota(jnp.int32, (c, c), 0)
    ci = lax.broadcasted_iota(jnp.int32, (c, c), 1)
    decay, eg, kb, k_bf, glast, strict = [], [], [], [], [], []
    vb, g_bs = [], []
    for q, k, v, gcol, bcol, grow, backward in chains:
        incl = ri <= ci if backward else ri >= ci
        strict.append(ri < ci if backward else ri > ci)
        g_bs.append(gcol)
        glast.append(gcol[0:1, :] if backward else gcol[c - 1:c, :])
        decay.append(jnp.where(incl, jnp.exp(gcol - grow), 0.0))
        eg.append(jnp.exp(gcol))
        kb.append(k * bcol)
        vb.append(v * bcol)
        k_bf.append(k.astype(BF16))
    kq = [_dot_nt(jnp.concatenate([kb[i].astype(BF16), chains[i][0].astype(BF16)], axis=0), k_bf[i])
          for i in range(n)]
    kk = [m[:c] for m in kq]
    qk = [m[c:] for m in kq]
    nm = [jnp.where(strict[i], -kk[i] * decay[i], 0.0) for i in range(n)]
    left = lax.broadcasted_iota(jnp.int32, (c, 2 * c), 1) < c
    eye_r = (lax.broadcasted_iota(jnp.int32, (c, 2 * c), 1) - c
             == lax.broadcasted_iota(jnp.int32, (c, 2 * c), 0)).astype(F32)
    nm_w = [jnp.concatenate([m, m], axis=1) for m in nm]
    sq = [_dot(nm[i].astype(BF16), nm_w[i].astype(BF16)) for i in range(n)]
    pt = [jnp.where(left, sq[i], eye_r + nm_w[i]) for i in range(n)]
    step = 2
    while step < c:
        res = [_dot(m[:, :c].astype(BF16), m.astype(BF16)) for m in pt]
        pt = [jnp.where(left, res[i], pt[i] + res[i]) for i in range(n)]
        step *= 2
    rhs = [_dot(pt[i][:, c:].astype(BF16),
                jnp.concatenate([vb[i], kb[i] * eg[i]], axis=1).astype(BF16))
           for i in range(n)]
    return [(rhs[i][:, :A_DV], rhs[i][:, A_DV:].astype(BF16), (qk[i] * decay[i]).astype(BF16),
             (chains[i][0] * eg[i]).astype(BF16),
             (chains[i][1] * jnp.exp(glast[i] - g_bs[i])).T.astype(BF16), jnp.exp(glast[i]))
            for i in range(n)]


def _delta_inter(intra, s_refs):
    n = len(intra)
    s_old = [r[...] for r in s_refs]
    s_bf = [s.astype(BF16) for s in s_old]
    c = A_CHUNK
    wq = [_dot(jnp.concatenate([intra[i][1], intra[i][3]], axis=0), s_bf[i]) for i in range(n)]
    ws = [m[:c] for m in wq]
    qs = [m[c:] for m in wq]
    v_new_bf = [(intra[i][0] - ws[i]).astype(BF16) for i in range(n)]
    av = [_dot(intra[i][2], v_new_bf[i]) for i in range(n)]
    kv = [_dot(intra[i][4], v_new_bf[i]) for i in range(n)]
    for i in range(n):
        s_refs[i][...] = s_old[i] * intra[i][5] + kv[i]
    return [qs[i] + av[i] for i in range(n)]


def _scan_kernel(qf_ref, kf_ref, vf_ref, gf_ref, qb_ref, kb_ref, vb_ref, gb_ref,
                 of_ref, ob_ref, s_ref):
    @pl.when(pl.program_id(1) == 0)
    def _():
        s_ref[...] = jnp.zeros_like(s_ref)

    tb = qf_ref.shape[1]
    nc = tb // A_CHUNK

    def body(ci, carry):
        chains, s_refs, dests = [], [], []
        for bb in range(qf_ref.shape[0]):
            for backward, (q_ref, k_ref, v_ref, g_ref, o_ref) in enumerate(
                    ((qf_ref, kf_ref, vf_ref, gf_ref, of_ref), (qb_ref, kb_ref, vb_ref, gb_ref, ob_ref))):
                cidx = nc - 1 - ci if backward else ci
                rows = pl.ds(pl.multiple_of(cidx * A_CHUNK, A_CHUNK), A_CHUNK)
                gtile = g_ref[bb, rows, :]
                gt = gtile.T
                for h in range(A_HEADS):
                    hs = slice(h * LANES, (h + 1) * LANES)
                    gl = backward * A_HEADS + h
                    bl = 2 * A_HEADS + gl
                    chains.append((q_ref[bb, rows, hs].astype(F32), k_ref[bb, rows, hs].astype(F32),
                                   v_ref[bb, rows, hs].astype(F32),
                                   gtile[:, gl:gl + 1], gtile[:, bl:bl + 1], gt[gl:gl + 1, :],
                                   bool(backward)))
                    s_refs.append(s_ref.at[bb * 2 * A_HEADS + gl])
                    dests.append((o_ref, bb, rows, hs))
        outs = _delta_inter(_delta_intra(chains), s_refs)
        for (o_ref, bb, rows, hs), o in zip(dests, outs):
            o_ref[bb, rows, hs] = o.astype(BF16)
        return carry

    lax.fori_loop(0, nc, body, 0)


def _scan(q3, k3, v3, gc3, tb, bb):
    b, l, _ = q3.shape
    nb = l // tb
    fwd = lambda bi, i: (bi, i, 0)
    bwd = lambda bi, i: (bi, nb - 1 - i, 0)
    big = lambda m: pl.BlockSpec((bb, tb, A_W), m)
    small = lambda m: pl.BlockSpec((bb, tb, LANES), m)
    o_shape = jax.ShapeDtypeStruct((b, l, A_HEADS * A_DV), BF16)
    return pl.pallas_call(
        _scan_kernel,
        grid=(b // bb, nb),
        in_specs=[big(fwd), big(fwd), big(fwd), small(fwd), big(bwd), big(bwd), big(bwd), small(bwd)],
        out_specs=[big(fwd), big(bwd)],
        out_shape=[o_shape, o_shape],
        scratch_shapes=[pltpu.VMEM((bb * 2 * A_HEADS, A_DK, A_DV), F32)],
        compiler_params=_params(("arbitrary", "arbitrary")),
    )(q3, k3, v3, gc3, q3, k3, v3, gc3)


def _attn_kernel(q_ref, qn_ref, k_ref, v_ref, gate_ref, o_ref, vx_ref, s_ref, m_ref, acc_ref, *, tk):
    l = k_ref.shape[1]
    tq = q_ref.shape[1]

    def kv_rows(t):
        if isinstance(t, int):
            return slice(t * tk, (t + 1) * tk)
        return pl.ds(pl.multiple_of(t * tk, tk), tk)

    def scores(src_ref, t, buf):
        k_blk = k_ref[0, kv_rows(t), :]
        for hh in range(B_GROUP):
            rows = slice(hh * tq, (hh + 1) * tq)
            s_ref[buf, rows, :] = _dot_nt(src_ref[0, :, hh * B_HD:(hh + 1) * B_HD], k_blk)

    @pl.when(pl.program_id(2) == 0)
    def _():
        def vbody(r, carry):
            sl = kv_rows(r)
            vx_ref[sl, :B_HD] = v_ref[0, sl, :]
            vx_ref[sl, B_HD:] = jnp.ones((tk, B_HD), BF16)
            return carry
        lax.fori_loop(0, l // tk, vbody, 0)
        scores(q_ref, 0, 0)

    m_ref[...] = jnp.full_like(m_ref, -jnp.inf)
    acc_ref[...] = jnp.zeros_like(acc_ref)

    def softmax_pv(t, buf):
        v_blk = vx_ref[kv_rows(t), :]
        for hh in range(B_GROUP):
            rows = slice(hh * tq, (hh + 1) * tq)
            s = s_ref[buf, rows, :]
            m_prev = m_ref[rows, :]
            m_new = jnp.maximum(m_prev, jnp.max(s, axis=-1, keepdims=True))
            alpha = jnp.exp2(m_prev - m_new)
            p = jnp.concatenate(
                [jnp.exp2(s[:, c * LANES:(c + 1) * LANES] - m_new) for c in range(tk // LANES)],
                axis=1).astype(BF16)
            acc_ref[rows, :] = jnp.concatenate([alpha, alpha], axis=1) * acc_ref[rows, :] + _dot(p, v_blk)
            m_ref[rows, :] = m_new

    n_kv = l // tk
    for t in range(n_kv - 1):
        scores(q_ref, t + 1, (t + 1) % 2)
        softmax_pv(t, t % 2)
    scores(qn_ref, 0, 0)
    softmax_pv(n_kv - 1, (n_kv - 1) % 2)

    for hh in range(B_GROUP):
        rows = slice(hh * tq, (hh + 1) * tq)
        o = acc_ref[rows, :B_HD] / acc_ref[rows, B_HD:]
        gate = gate_ref[0, :, hh * B_HD:(hh + 1) * B_HD].astype(F32)
        o_ref[0, :, hh * B_HD:(hh + 1) * B_HD] = (o * gate).astype(BF16)


def _attn(p3, tq, tk):
    b, l, _ = p3.shape
    gw = B_GROUP * B_HD
    nq = l // tq
    return pl.pallas_call(
        functools.partial(_attn_kernel, tk=tk),
        grid=(b, B_KV_HEADS, nq),
        in_specs=[
            pl.BlockSpec((1, tq, gw), lambda bi, kv, i: (bi, i, COL_BQ // gw + kv)),
            pl.BlockSpec((1, tq, gw), lambda bi, kv, i: (bi, jnp.minimum(i + 1, nq - 1), COL_BQ // gw + kv)),
            pl.BlockSpec((1, l, B_HD), lambda bi, kv, i: (bi, 0, COL_BK // B_HD + kv)),
            pl.BlockSpec((1, l, B_HD), lambda bi, kv, i: (bi, 0, COL_BV // B_HD + kv)),
            pl.BlockSpec((1, tq, gw), lambda bi, kv, i: (bi, i, COL_BGATE // gw + kv)),
        ],
        out_specs=pl.BlockSpec((1, tq, gw), lambda bi, kv, i: (bi, i, kv)),
        out_shape=jax.ShapeDtypeStruct((b, l, B_W), BF16),
        scratch_shapes=[
            pltpu.VMEM((l, 2 * B_HD), BF16),
            pltpu.VMEM((2, B_GROUP * tq, tk), F32),
            pltpu.VMEM((B_GROUP * tq, B_HD), F32),
            pltpu.VMEM((B_GROUP * tq, 2 * B_HD), F32),
        ],
        compiler_params=_params(("arbitrary", "arbitrary", "arbitrary")),
    )(p3, p3, p3, p3, p3)


def _next_head_spec(tm, width, col, n_rows):
    last = n_rows // SUB_ROWS - 1
    per = tm // SUB_ROWS
    return pl.BlockSpec((SUB_ROWS, width), lambda i: (jnp.minimum((i + 1) * per, last), col))


def _out0_prologue(of_ref, ob_ref, ag_ref, yb_ref, og_ref, rows, y_buf):
    aw = A_HEADS * A_DV
    for h in range(A_HEADS):
        hs = slice(h * A_DV, (h + 1) * A_DV)
        o = of_ref[rows, hs].astype(F32) + ob_ref[rows, hs].astype(F32)
        o = o * lax.rsqrt(jnp.mean(o * o, axis=-1, keepdims=True) + EPS) * og_ref[...]
        y_buf[:, hs] = (o * ag_ref[rows, hs].astype(F32)).astype(BF16)
    y_buf[:, aw:] = yb_ref[rows, :]


def _out0_kernel(of_ref, ob_ref, ag_ref, yb_ref, og_ref, w_ref, x_ref, g1_ref, o_ref, h1_ref, y_ref):
    d = w_ref.shape[1]
    for sub in range(x_ref.shape[0] // SUB_ROWS):
        rows = slice(sub * SUB_ROWS, (sub + 1) * SUB_ROWS)
        y_buf = y_ref.at[sub % 2]
        _out0_prologue(of_ref, ob_ref, ag_ref, yb_ref, og_ref, rows, y_buf)
        for c0 in range(0, d, COL_CHUNK):
            cs = slice(c0, c0 + COL_CHUNK)
            o_ref[rows, cs] = x_ref[rows, cs] + _dot(y_buf[...], w_ref[:, cs])
        _rmsnorm_rows(o_ref, g1_ref, h1_ref.at[rows], sub * SUB_ROWS, SUB_ROWS)


def _out0(o_f, o_b, p2, yb, og, w_out, x2, g1, tm):
    t, d = x2.shape
    aw = A_HEADS * A_DV
    kdim = aw + B_W
    return pl.pallas_call(
        _out0_kernel,
        grid=(t // tm,),
        in_specs=[
            pl.BlockSpec((tm, aw), lambda i: (i, 0)),
            pl.BlockSpec((tm, aw), lambda i: (i, 0)),
            pl.BlockSpec((tm, aw), lambda i: (i, COL_AGATE // aw)),
            pl.BlockSpec((tm, B_W), lambda i: (i, 0)),
            _resident((1, A_DV)),
            _resident((kdim, d)),
            pl.BlockSpec((tm, d), lambda i: (i, 0)),
            _resident((1, d)),
        ],
        out_specs=[pl.BlockSpec((tm, d), lambda i: (i, 0)), pl.BlockSpec((tm, d), lambda i: (i, 0))],
        out_shape=[jax.ShapeDtypeStruct((t, d), F32), jax.ShapeDtypeStruct((t, d), BF16)],
        scratch_shapes=[pltpu.VMEM((2, SUB_ROWS, kdim), BF16)],
        compiler_params=_params(("arbitrary",)),
    )(o_f, o_b, p2, yb, og, w_out, x2, g1)


def _gelu(x):
    return 0.5 * x * (1.0 + lax.erf(x * (2.0 ** -0.5)))


def _proj1_kernel(h_ref, w_ref, p_ref):
    n = w_ref.shape[1]
    for sub in range(h_ref.shape[0] // SUB_ROWS):
        rows = slice(sub * SUB_ROWS, (sub + 1) * SUB_ROWS)
        for c0 in range(0, n, COL_CHUNK):
            cs = slice(c0, c0 + COL_CHUNK)
            y = _dot(h_ref[rows, :], w_ref[:, cs])
            act = _gelu(y) if c0 < 2 * n // 3 else y * jax.nn.sigmoid(y)
            p_ref[rows, cs] = act.astype(BF16)


def _proj1(h2, w, tm):
    t, d = h2.shape
    n = w.shape[1]
    return pl.pallas_call(
        _proj1_kernel,
        grid=(t // tm,),
        in_specs=[pl.BlockSpec((tm, d), lambda i: (i, 0)), _resident((d, n))],
        out_specs=pl.BlockSpec((tm, n), lambda i: (i, 0)),
        out_shape=jax.ShapeDtypeStruct((t, n), BF16),
        compiler_params=_params(("arbitrary",)),
    )(h2, w)


def _out1_prologue(u_ref, v_ref, gt_ref, lng_ref, lnb_ref, ws_ref, bst_ref, row0, z_buf):
    gw = v_ref.shape[1] // C_GROUPS
    for c in range(SUB_ROWS // C_CHUNK):
        rows = slice(row0 + c * C_CHUNK, row0 + (c + 1) * C_CHUNK)
        zrows = slice(c * C_CHUNK, (c + 1) * C_CHUNK)
        v = v_ref[rows, :].astype(F32)
        mu = jnp.mean(v, axis=-1, keepdims=True)
        vc = v - mu
        var = jnp.mean(vc * vc, axis=-1, keepdims=True)
        vn = (vc * lax.rsqrt(var + EPS) * lng_ref[...] + lnb_ref[...]).astype(BF16)
        for g in range(C_GROUPS):
            gs = slice(g * gw, (g + 1) * gw)
            s = _dot(ws_ref[g], vn[:, gs]) + bst_ref[:, g:g + 1]
            z = u_ref[rows, gs].astype(F32) * s * gt_ref[rows, gs].astype(F32)
            z_buf[zrows, gs] = z.astype(BF16)


def _out1_kernel(u_ref, v_ref, gt_ref, un_ref, vn_ref, gtn_ref, lng_ref, lnb_ref, ws_ref, bst_ref, w_ref,
                 x_ref, o_ref, z_ref):
    d = w_ref.shape[1]
    n_sub = x_ref.shape[0] // SUB_ROWS
    params = (lng_ref, lnb_ref, ws_ref, bst_ref)

    @pl.when(pl.program_id(0) == 0)
    def _():
        _out1_prologue(u_ref, v_ref, gt_ref, *params, 0, z_ref.at[0])

    for sub in range(n_sub):
        rows = slice(sub * SUB_ROWS, (sub + 1) * SUB_ROWS)
        z_buf = z_ref.at[sub % 2]
        for c0 in range(0, d, COL_CHUNK):
            cs = slice(c0, c0 + COL_CHUNK)
            o_ref[rows, cs] = x_ref[rows, cs] + _dot(z_buf[...], w_ref[:, cs])
        if sub + 1 < n_sub:
            _out1_prologue(u_ref, v_ref, gt_ref, *params, (sub + 1) * SUB_ROWS, z_ref.at[(sub + 1) % 2])
        else:
            _out1_prologue(un_ref, vn_ref, gtn_ref, *params, 0, z_ref.at[0])


def _out1(p2, lng, lnb, ws, bst, w_out, x2, tm):
    t, d = x2.shape
    cw = w_out.shape[0]
    assert (tm // SUB_ROWS) % 2 == 0
    return pl.pallas_call(
        _out1_kernel,
        grid=(t // tm,),
        in_specs=[
            pl.BlockSpec((tm, cw), lambda i: (i, 0)),
            pl.BlockSpec((tm, cw), lambda i: (i, 1)),
            pl.BlockSpec((tm, cw), lambda i: (i, 2)),
            _next_head_spec(tm, cw, 0, t),
            _next_head_spec(tm, cw, 1, t),
            _next_head_spec(tm, cw, 2, t),
            _resident((1, cw)),
            _resident((1, cw)),
            _resident((C_GROUPS, C_CHUNK, C_CHUNK)),
            _resident((C_CHUNK, C_GROUPS)),
            _resident((cw, d)),
            pl.BlockSpec((tm, d), lambda i: (i, 0)),
        ],
        out_specs=pl.BlockSpec((tm, d), lambda i: (i, 0)),
        out_shape=jax.ShapeDtypeStruct((t, d), F32),
        scratch_shapes=[pltpu.VMEM((2, SUB_ROWS, cw), BF16)],
        compiler_params=_params(("arbitrary",)),
    )(p2, p2, p2, p2, p2, p2, lng, lnb, ws, bst, w_out, x2)


def _rope_tables(l):
    t = jnp.arange(l)
    row = (t // GRID_W).astype(F32)
    col = (t % GRID_W).astype(F32)
    n_freq = B_HD // 4
    inv = ROPE_THETA ** (-jnp.arange(n_freq, dtype=F32) / n_freq)
    ar = row[:, None] * inv
    ac = col[:, None] * inv
    cos = jnp.concatenate([jnp.cos(ar), jnp.cos(ar), jnp.cos(ac), jnp.cos(ac)], axis=-1)
    sin = jnp.concatenate([-jnp.sin(ar), jnp.sin(ar), -jnp.sin(ac), jnp.sin(ac)], axis=-1)
    return cos, sin


def _lane_row(v):
    v = v.reshape(1, -1).astype(F32)
    return jnp.pad(v, ((0, 0), (0, LANES - v.shape[1])))


def _pick(n, pref):
    while n % pref:
        pref //= 2
    return pref


def _hybrid_layer(x2, b, l, norm_g, w_in, conv_w, a_log, dt_bias, onorm_g, qn_g, kn_g, w_out, next_norm_g):
    t, d = x2.shape
    sizes = (A_W, A_W, A_HEADS * A_DV, A_HEADS * A_DV, 4 * A_HEADS, B_W, B_KV_W, B_KV_W, B_W)
    offs = [0]
    for s in sizes:
        offs.append(offs[-1] + s)
    aq, ak, av, agate, alog_in, bq, bk, bv, bgate = [w_in[:, offs[i]:offs[i + 1]] for i in range(9)]
    w_main = jnp.concatenate([aq, ak, av, agate, bq, bgate, bk, bv], axis=1).astype(BF16)
    w_log = jnp.pad(alog_in, ((0, 0), (0, LANES - alog_in.shape[1]))).astype(BF16)

    tm = _pick(math.gcd(t, l), 512)
    cos, sin_signed = _rope_tables(l)
    p2, lg2 = _proj0(x2, norm_g.reshape(1, d), w_main, w_log, cos, sin_signed,
                     qn_g.reshape(1, B_HD).astype(F32), kn_g.reshape(1, B_HD).astype(F32), tm)
    p3 = p2.reshape(b, l, P0_COLS)
    lg3 = lg2.reshape(b, l, LANES)

    tb = _pick(l, 512)
    q3, k3, v3, gc3 = _prep(p3, lg3, conv_w.astype(F32), _lane_row(a_log), _lane_row(dt_bias), tb)
    o_f, o_b = _scan(q3, k3, v3, gc3, tb, 2 if b % 2 == 0 else 1)

    yb = _attn(p3, _pick(l, 256), _pick(l, 512))

    return _out0(o_f.reshape(t, -1), o_b.reshape(t, -1), p2, yb.reshape(t, B_W),
                 onorm_g.reshape(1, A_DV).astype(F32), w_out.astype(BF16), x2,
                 next_norm_g.reshape(1, d).astype(F32), tm)


def _gmlp_layer(x2, h2, w_in, ln_g, ln_b, w_s, b_s, w_out):
    t, d = x2.shape
    cw = w_out.shape[0]
    tm = _pick(t, 512)
    p2 = _proj1(h2, w_in.astype(BF16), tm)
    return _out1(p2, ln_g.reshape(1, cw).astype(F32), ln_b.reshape(1, cw).astype(F32),
                 w_s.astype(BF16), jnp.transpose(b_s).astype(F32), w_out.astype(BF16), x2, tm)


def kernel(x, norm0_g, w_in0, conv0_w, a_log0, dt_bias0, a_onorm_g0, b_qnorm_g0, b_knorm_g0, w_out0,
           norm1_g, w_in1, c_ln_g1, c_ln_b1, c_ws1, c_bs1, w_out1):
    b, l, d = x.shape
    assert w_in0.shape[0] == 1 and w_in1.shape[0] == 1
    x2 = x.reshape(b * l, d)
    x2, h2 = _hybrid_layer(x2, b, l, norm0_g[0], w_in0[0], conv0_w[0], a_log0[0], dt_bias0[0],
                           a_onorm_g0[0], b_qnorm_g0[0], b_knorm_g0[0], w_out0[0], norm1_g[0])
    x2 = _gmlp_layer(x2, h2, w_in1[0], c_ln_g1[0], c_ln_b1[0], c_ws1[0], c_bs1[0], w_out1[0])
    return x2.reshape(b, l, d)
```

```python
import functools
import math

import jax
import jax.numpy as jnp
from jax import lax
from jax.experimental import pallas as pl
from jax.experimental.pallas import tpu as pltpu

F32 = jnp.float32
BF16 = jnp.bfloat16

EPS = 1e-6
GRID_W = 64
ROPE_THETA = 10000.0
A_HEADS = 8
A_DK = 128
A_DV = 128
A_CHUNK = 64
CONV_W = 5
A_W = A_HEADS * A_DK
B_HEADS = 8
B_KV_HEADS = 2
B_HD = 128
B_GROUP = B_HEADS // B_KV_HEADS
B_W = B_HEADS * B_HD
B_KV_W = B_KV_HEADS * B_HD
C_GROUPS = 16
C_CHUNK = 128

LANES = 128
VMEM_LIMIT = 56 * 1024 * 1024

COL_AQ = 0
COL_AK = A_W
COL_AV = 2 * A_W
COL_AGATE = 3 * A_W
COL_BQ = 4 * A_W
COL_BGATE = COL_BQ + B_W
COL_BK = COL_BGATE + B_W
COL_BV = COL_BK + B_KV_W
P0_COLS = COL_BV + B_KV_W


def _params(sem):
    return pltpu.CompilerParams(dimension_semantics=sem, vmem_limit_bytes=VMEM_LIMIT)


def _dot(a, b):
    return jnp.dot(a, b, preferred_element_type=F32)


def _dot_nt(a, b):
    return lax.dot_general(a, b, (((1,), (1,)), ((), ())), preferred_element_type=F32)


SUB_ROWS = 256
COL_CHUNK = 512
NORM_ROWS = 64


def _resident(shape):
    nd = len(shape)
    return pl.BlockSpec(shape, lambda *_: (0,) * nd, pipeline_mode=pl.Buffered(1))


def _rmsnorm_rows(x_ref, g_ref, h_ref, row0, nrows):
    for r in range(0, nrows, NORM_ROWS):
        x = x_ref[row0 + r:row0 + r + NORM_ROWS, :]
        ms = jnp.mean(x * x, axis=-1, keepdims=True)
        h_ref[r:r + NORM_ROWS, :] = (x * lax.rsqrt(ms + EPS) * g_ref[...]).astype(BF16)


def _silu(x):
    return x * jax.nn.sigmoid(x)


def _norm_rope(x, g, cos, sin_signed):
    y = x * lax.rsqrt(jnp.mean(x * x, axis=-1, keepdims=True) + EPS) * g
    lane = lax.broadcasted_iota(jnp.int32, y.shape, 1)
    quarter = B_HD // 4
    partner = jnp.where(lane % (2 * quarter) < quarter,
                        pltpu.roll(y, B_HD - quarter, axis=1), pltpu.roll(y, quarter, axis=1))
    return y * cos + partner * sin_signed


Q_SCALE = (B_HD ** -0.5) * math.log2(math.e)


def _proj0_epilogue(y, c0, cos, sin_signed, qg, kg):
    if COL_AGATE <= c0 < COL_BQ or COL_BGATE <= c0 < COL_BK:
        return _silu(y)
    if COL_BQ <= c0 < COL_BGATE or COL_BK <= c0 < COL_BV:
        heads = []
        for h0 in range(0, COL_CHUNK, B_HD):
            yh = y[:, h0:h0 + B_HD]
            if COL_BQ <= c0 + h0 < COL_BGATE:
                yh = _norm_rope(yh, qg, cos, sin_signed) * Q_SCALE
            elif COL_BK <= c0 + h0 < COL_BV:
                yh = _norm_rope(yh, kg, cos, sin_signed)
            heads.append(yh)
        return jnp.concatenate(heads, axis=1)
    return y


def _proj0_kernel(x_ref, g_ref, w_ref, wl_ref, cos_ref, sin_ref, qg_ref, kg_ref, p_ref, lg_ref, h_ref):
    n = w_ref.shape[1]
    for sub in range(x_ref.shape[0] // SUB_ROWS):
        h_buf = h_ref.at[sub % 2]
        _rmsnorm_rows(x_ref, g_ref, h_buf, sub * SUB_ROWS, SUB_ROWS)
        rows = slice(sub * SUB_ROWS, (sub + 1) * SUB_ROWS)
        lg_ref[rows, :] = _dot(h_buf[...], wl_ref[...])
        for c0 in range(0, n, COL_CHUNK):
            y = _dot(h_buf[...], w_ref[:, c0:c0 + COL_CHUNK])
            y = _proj0_epilogue(y, c0, cos_ref[rows, :], sin_ref[rows, :], qg_ref[...], kg_ref[...])
            p_ref[rows, c0:c0 + COL_CHUNK] = y.astype(BF16)


def _proj0(x2, g, w_main, w_log, cos, sin_signed, qg, kg, tm):
    t, d = x2.shape
    n = w_main.shape[1]
    l = cos.shape[0]
    assert l % tm == 0
    return pl.pallas_call(
        _proj0_kernel,
        grid=(t // tm,),
        in_specs=[
            pl.BlockSpec((tm, d), lambda i: (i, 0)),
            _resident((1, d)),
            _resident((d, n)),
            _resident((d, LANES)),
            pl.BlockSpec((tm, B_HD), lambda i: (i % (l // tm), 0)),
            pl.BlockSpec((tm, B_HD), lambda i: (i % (l // tm), 0)),
            _resident((1, B_HD)),
            _resident((1, B_HD)),
        ],
        out_specs=[
            pl.BlockSpec((tm, n), lambda i: (i, 0)),
            pl.BlockSpec((tm, LANES), lambda i: (i, 0)),
        ],
        out_shape=[
            jax.ShapeDtypeStruct((t, n), BF16),
            jax.ShapeDtypeStruct((t, LANES), F32),
        ],
        scratch_shapes=[pltpu.VMEM((2, SUB_ROWS, d), BF16)],
        compiler_params=_params(("arbitrary",)),
    )(x2, g, w_main, w_log, cos, sin_signed, qg, kg)


HALO = 16
CONV_ROWS = 128


def _softplus(x):
    return jnp.maximum(x, 0.0) + jnp.log1p(jnp.exp(-jnp.abs(x)))


def _prep_kernel(main_ref, hb_ref, ha_ref, cw_ref, lg_ref, alog_ref, dtb_ref,
                 q_ref, k_ref, v_ref, gc_ref):
    i = pl.program_id(1)
    first = i == 0
    last = i == pl.num_programs(1) - 1
    tb = main_ref.shape[1]
    taps = [j for j in range(CONV_W) if j != CONV_W // 2]
    kwin = CONV_ROWS + 2 * HALO
    r = lax.broadcasted_iota(jnp.int32, (len(taps) * CONV_ROWS, kwin), 0)
    col = lax.broadcasted_iota(jnp.int32, (len(taps) * CONV_ROWS, kwin), 1)
    tap_idx = r // CONV_ROWS
    tap = tap_idx + (tap_idx >= CONV_W // 2).astype(jnp.int32)
    select = (col == r % CONV_ROWS + tap + (HALO - CONV_W // 2)).astype(BF16)
    for rb in range(tb // CONV_ROWS):
        r0 = rb * CONV_ROWS
        rows = slice(r0, r0 + CONV_ROWS)
        for n0 in range(0, 3 * A_W, COL_CHUNK):
            ns = slice(n0, n0 + COL_CHUNK)
            if rb == 0:
                before = jnp.where(first, jnp.zeros((), BF16), hb_ref[0, :, ns])
            else:
                before = main_ref[0, r0 - HALO:r0, ns]
            if r0 + CONV_ROWS == tb:
                after = jnp.where(last, jnp.zeros((), BF16), ha_ref[0, :, ns])
            else:
                after = main_ref[0, r0 + CONV_ROWS:r0 + CONV_ROWS + HALO, ns]
            mid = main_ref[0, rows, ns]
            shifted = _dot(select, jnp.concatenate([before, mid, after], axis=0))
            for l0 in range(0, COL_CHUNK, LANES):
                s = (n0 + l0) // LANES
                cs = slice(n0 + l0, n0 + l0 + LANES)
                acc = mid[:, l0:l0 + LANES].astype(F32) * cw_ref[CONV_W // 2:CONV_W // 2 + 1, cs]
                for ti, j in enumerate(taps):
                    acc = acc + shifted[ti * CONV_ROWS:(ti + 1) * CONV_ROWS, l0:l0 + LANES] * cw_ref[j:j + 1, cs]
                y = acc * jax.nn.sigmoid(acc)
                head = s % A_HEADS
                hs = slice(head * LANES, (head + 1) * LANES)
                if s < A_HEADS:
                    y = y * lax.rsqrt(jnp.sum(y * y, axis=-1, keepdims=True) + EPS) * (A_DK ** -0.5)
                    q_ref[0, rows, hs] = y.astype(BF16)
                elif s < 2 * A_HEADS:
                    y = y * lax.rsqrt(jnp.sum(y * y, axis=-1, keepdims=True) + EPS)
                    k_ref[0, rows, hs] = y.astype(BF16)
                else:
                    v_ref[0, rows, hs] = y.astype(BF16)

    lg = lg_ref[0]
    g = -jnp.exp(alog_ref[...]) * _softplus(lg + dtb_ref[...])
    beta = jax.nn.sigmoid(lg)
    row = lax.broadcasted_iota(jnp.int32, (tb, LANES), 0) % A_CHUNK
    lane = lax.broadcasted_iota(jnp.int32, (tb, LANES), 1)
    pf = g
    sf = g
    step = 1
    while step < A_CHUNK:
        pf = pf + jnp.where(row >= step, pltpu.roll(pf, step, axis=0), 0.0)
        sf = sf + jnp.where(row < A_CHUNK - step, pltpu.roll(sf, tb - step, axis=0), 0.0)
        step *= 2
    gc_ref[0] = jnp.where(lane < A_HEADS, pf, jnp.where(lane < 2 * A_HEADS, sf, beta))


def _prep(p3, lg3, conv_w, alog_l, dtb_l, tb):
    b, l, _ = p3.shape
    nb = l // tb
    hpb = tb // HALO
    qkv_shape = jax.ShapeDtypeStruct((b, l, A_W), BF16)
    return pl.pallas_call(
        _prep_kernel,
        grid=(b, nb),
        in_specs=[
            pl.BlockSpec((1, tb, 3 * A_W), lambda bi, i: (bi, i, 0)),
            pl.BlockSpec((1, HALO, 3 * A_W), lambda bi, i: (bi, jnp.maximum(i * hpb - 1, 0), 0)),
            pl.BlockSpec((1, HALO, 3 * A_W),
                         lambda bi, i: (bi, jnp.minimum((i + 1) * hpb, l // HALO - 1), 0)),
            pl.BlockSpec((CONV_W, 3 * A_W), lambda bi, i: (0, 0)),
            pl.BlockSpec((1, tb, LANES), lambda bi, i: (bi, i, 0)),
            pl.BlockSpec((1, LANES), lambda bi, i: (0, 0)),
            pl.BlockSpec((1, LANES), lambda bi, i: (0, 0)),
        ],
        out_specs=[
            pl.BlockSpec((1, tb, A_W), lambda bi, i: (bi, i, 0)),
            pl.BlockSpec((1, tb, A_W), lambda bi, i: (bi, i, 0)),
            pl.BlockSpec((1, tb, A_W), lambda bi, i: (bi, i, 0)),
            pl.BlockSpec((1, tb, LANES), lambda bi, i: (bi, i, 0)),
        ],
        out_shape=[qkv_shape, qkv_shape, qkv_shape, jax.ShapeDtypeStruct((b, l, LANES), F32)],
        compiler_params=_params(("arbitrary", "arbitrary")),
    )(p3, p3, p3, conv_w, lg3, alog_l, dtb_l)


def _delta_intra(chains):
    c = A_CHUNK
    n = len(chains)
    ri = lax.broadcasted_iota(jnp.int32, (c, c), 0)
    ci = lax.broadcasted_iota(jnp.int32, (c, c), 1)
    decay, eg, kb, k_bf, glast, strict = [], [], [], [], [], []
    vb, g_bs = [], []
    for q, k, v, gcol, bcol, grow, backward in chains:
        incl = ri <= ci if backward else ri >= ci
        strict.append(ri < ci if backward else ri > ci)
        g_bs.append(gcol)
        glast.append(gcol[0:1, :] if backward else gcol[c - 1:c, :])
        decay.append(jnp.where(incl, jnp.exp(gcol - grow), 0.0))
        eg.append(jnp.exp(gcol))
        kb.append(k * bcol)
        vb.append(v * bcol)
        k_bf.append(k.astype(BF16))
    kq = [_dot_nt(jnp.concatenate([kb[i].astype(BF16), chains[i][0].astype(BF16)], axis=0), k_bf[i])
          for i in range(n)]
    kk = [m[:c] for m in kq]
    qk = [m[c:] for m in kq]
    nm = [jnp.where(strict[i], -kk[i] * decay[i], 0.0) for i in range(n)]
    left = lax.broadcasted_iota(jnp.int32, (c, 2 * c), 1) < c
    eye_r = (lax.broadcasted_iota(jnp.int32, (c, 2 * c), 1) - c
             == lax.broadcasted_iota(jnp.int32, (c, 2 * c), 0)).astype(F32)
    nm_w = [jnp.concatenate([m, m], axis=1) for m in nm]
    sq = [_dot(nm[i].astype(BF16), nm_w[i].astype(BF16)) for i in range(n)]
    pt = [jnp.where(left, sq[i], eye_r + nm_w[i]) for i in range(n)]
    step = 2
    while step < c:
        res = [_dot(m[:, :c].astype(BF16), m.astype(BF16)) for m in pt]
        pt = [jnp.where(left, res[i], pt[i] + res[i]) for i in range(n)]
        step *= 2
    rhs = [_dot(pt[i][:, c:].astype(BF16),
                jnp.concatenate([vb[i], kb[i] * eg[i]], axis=1).astype(BF16))
           for i in range(n)]
    return [(rhs[i][:, :A_DV], rhs[i][:, A_DV:].astype(BF16), (qk[i] * decay[i]).astype(BF16),
             (chains[i][0] * eg[i]).astype(BF16),
             (chains[i][1] * jnp.exp(glast[i] - g_bs[i])).T.astype(BF16), jnp.exp(glast[i]))
            for i in range(n)]


def _delta_inter(intra, s_refs):
    n = len(intra)
    s_old = [r[...] for r in s_refs]
    s_bf = [s.astype(BF16) for s in s_old]
    c = A_CHUNK
    wq = [_dot(jnp.concatenate([intra[i][1], intra[i][3]], axis=0), s_bf[i]) for i in range(n)]
    ws = [m[:c] for m in wq]
    qs = [m[c:] for m in wq]
    v_new_bf = [(intra[i][0] - ws[i]).astype(BF16) for i in range(n)]
    av = [_dot(intra[i][2], v_new_bf[i]) for i in range(n)]
    kv = [_dot(intra[i][4], v_new_bf[i]) for i in range(n)]
    for i in range(n):
        s_refs[i][...] = s_old[i] * intra[i][5] + kv[i]
    return [qs[i] + av[i] for i in range(n)]


def _scan_kernel(qf_ref, kf_ref, vf_ref, gf_ref, qb_ref, kb_ref, vb_ref, gb_ref,
                 of_ref, ob_ref, s_ref):
    @pl.when(pl.program_id(1) == 0)
    def _():
        s_ref[...] = jnp.zeros_like(s_ref)

    tb = qf_ref.shape[1]
    nc = tb // A_CHUNK

    def body(ci, carry):
        chains, s_refs, dests = [], [], []
        for bb in range(qf_ref.shape[0]):
            for backward, (q_ref, k_ref, v_ref, g_ref, o_ref) in enumerate(
                    ((qf_ref, kf_ref, vf_ref, gf_ref, of_ref), (qb_ref, kb_ref, vb_ref, gb_ref, ob_ref))):
                cidx = nc - 1 - ci if backward else ci
                rows = pl.ds(pl.multiple_of(cidx * A_CHUNK, A_CHUNK), A_CHUNK)
                gtile = g_ref[bb, rows, :]
                gt = gtile.T
                for h in range(A_HEADS):
                    hs = slice(h * LANES, (h + 1) * LANES)
                    gl = backward * A_HEADS + h
                    bl = 2 * A_HEADS + gl
                    chains.append((q_ref[bb, rows, hs].astype(F32), k_ref[bb, rows, hs].astype(F32),
                                   v_ref[bb, rows, hs].astype(F32),
                                   gtile[:, gl:gl + 1], gtile[:, bl:bl + 1], gt[gl:gl + 1, :],
                                   bool(backward)))
                    s_refs.append(s_ref.at[bb * 2 * A_HEADS + gl])
                    dests.append((o_ref, bb, rows, hs))
        outs = _delta_inter(_delta_intra(chains), s_refs)
        for (o_ref, bb, rows, hs), o in zip(dests, outs):
            o_ref[bb, rows, hs] = o.astype(BF16)
        return carry

    lax.fori_loop(0, nc, body, 0)


def _scan(q3, k3, v3, gc3, tb, bb):
    b, l, _ = q3.shape
    nb = l // tb
    fwd = lambda bi, i: (bi, i, 0)
    bwd = lambda bi, i: (bi, nb - 1 - i, 0)
    big = lambda m: pl.BlockSpec((bb, tb, A_W), m)
    small = lambda m: pl.BlockSpec((bb, tb, LANES), m)
    o_shape = jax.ShapeDtypeStruct((b, l, A_HEADS * A_DV), BF16)
    return pl.pallas_call(
        _scan_kernel,
        grid=(b // bb, nb),
        in_specs=[big(fwd), big(fwd), big(fwd), small(fwd), big(bwd), big(bwd), big(bwd), small(bwd)],
        out_specs=[big(fwd), big(bwd)],
        out_shape=[o_shape, o_shape],
        scratch_shapes=[pltpu.VMEM((bb * 2 * A_HEADS, A_DK, A_DV), F32)],
        compiler_params=_params(("arbitrary", "arbitrary")),
    )(q3, k3, v3, gc3, q3, k3, v3, gc3)


def _attn_kernel(q_ref, qn_ref, k_ref, v_ref, gate_ref, o_ref, vx_ref, s_ref, m_ref, acc_ref, *, tk):
    l = k_ref.shape[1]
    tq = q_ref.shape[1]

    def kv_rows(t):
        if isinstance(t, int):
            return slice(t * tk, (t + 1) * tk)
        return pl.ds(pl.multiple_of(t * tk, tk), tk)

    def scores(src_ref, t, buf):
        k_blk = k_ref[0, kv_rows(t), :]
        for hh in range(B_GROUP):
            rows = slice(hh * tq, (hh + 1) * tq)
            s_ref[buf, rows, :] = _dot_nt(src_ref[0, :, hh * B_HD:(hh + 1) * B_HD], k_blk)

    @pl.when(pl.program_id(2) == 0)
    def _():
        def vbody(r, carry):
            sl = kv_rows(r)
            vx_ref[sl, :B_HD] = v_ref[0, sl, :]
            vx_ref[sl, B_HD:] = jnp.ones((tk, B_HD), BF16)
            return carry
        lax.fori_loop(0, l // tk, vbody, 0)
        scores(q_ref, 0, 0)

    m_ref[...] = jnp.full_like(m_ref, -jnp.inf)
    acc_ref[...] = jnp.zeros_like(acc_ref)

    def softmax_pv(t, buf):
        v_blk = vx_ref[kv_rows(t), :]
        for hh in range(B_GROUP):
            rows = slice(hh * tq, (hh + 1) * tq)
            s = s_ref[buf, rows, :]
            m_prev = m_ref[rows, :]
            m_new = jnp.maximum(m_prev, jnp.max(s, axis=-1, keepdims=True))
            alpha = jnp.exp2(m_prev - m_new)
            p = jnp.concatenate(
                [jnp.exp2(s[:, c * LANES:(c + 1) * LANES] - m_new) for c in range(tk // LANES)],
                axis=1).astype(BF16)
            acc_ref[rows, :] = jnp.concatenate([alpha, alpha], axis=1) * acc_ref[rows, :] + _dot(p, v_blk)
            m_ref[rows, :] = m_new

    n_kv = l // tk
    for t in range(n_kv - 1):
        scores(q_ref, t + 1, (t + 1) % 2)
        softmax_pv(t, t % 2)
    scores(qn_ref, 0, 0)
    softmax_pv(n_kv - 1, (n_kv - 1) % 2)

    for hh in range(B_GROUP):
        rows = slice(hh * tq, (hh + 1) * tq)
        o = acc_ref[rows, :B_HD] / acc_ref[rows, B_HD:]
        gate = gate_ref[0, :, hh * B_HD:(hh + 1) * B_HD].astype(F32)
        o_ref[0, :, hh * B_HD:(hh + 1) * B_HD] = (o * gate).astype(BF16)


def _attn(p3, tq, tk):
    b, l, _ = p3.shape
    gw = B_GROUP * B_HD
    nq = l // tq
    return pl.pallas_call(
        functools.partial(_attn_kernel, tk=tk),
        grid=(b, B_KV_HEADS, nq),
        in_specs=[
            pl.BlockSpec((1, tq, gw), lambda bi, kv, i: (bi, i, COL_BQ // gw + kv)),
            pl.BlockSpec((1, tq, gw), lambda bi, kv, i: (bi, jnp.minimum(i + 1, nq - 1), COL_BQ // gw + kv)),
            pl.BlockSpec((1, l, B_HD), lambda bi, kv, i: (bi, 0, COL_BK // B_HD + kv)),
            pl.BlockSpec((1, l, B_HD), lambda bi, kv, i: (bi, 0, COL_BV // B_HD + kv)),
            pl.BlockSpec((1, tq, gw), lambda bi, kv, i: (bi, i, COL_BGATE // gw + kv)),
        ],
        out_specs=pl.BlockSpec((1, tq, gw), lambda bi, kv, i: (bi, i, kv)),
        out_shape=jax.ShapeDtypeStruct((b, l, B_W), BF16),
        scratch_shapes=[
            pltpu.VMEM((l, 2 * B_HD), BF16),
            pltpu.VMEM((2, B_GROUP * tq, tk), F32),
            pltpu.VMEM((B_GROUP * tq, B_HD), F32),
            pltpu.VMEM((B_GROUP * tq, 2 * B_HD), F32),
        ],
        compiler_params=_params(("arbitrary", "arbitrary", "arbitrary")),
    )(p3, p3, p3, p3, p3)


def _next_head_spec(tm, width, col, n_rows):
    last = n_rows // SUB_ROWS - 1
    per = tm // SUB_ROWS
    return pl.BlockSpec((SUB_ROWS, width), lambda i: (jnp.minimum((i + 1) * per, last), col))


def _out0_prologue(of_ref, ob_ref, ag_ref, yb_ref, og_ref, rows, y_buf):
    aw = A_HEADS * A_DV
    for h in range(A_HEADS):
        hs = slice(h * A_DV, (h + 1) * A_DV)
        o = of_ref[rows, hs].astype(F32) + ob_ref[rows, hs].astype(F32)
        o = o * lax.rsqrt(jnp.mean(o * o, axis=-1, keepdims=True) + EPS) * og_ref[...]
        y_buf[:, hs] = (o * ag_ref[rows, hs].astype(F32)).astype(BF16)
    y_buf[:, aw:] = yb_ref[rows, :]


def _out0_kernel(of_ref, ob_ref, ag_ref, yb_ref, og_ref, w_ref, x_ref, g1_ref, o_ref, h1_ref, y_ref):
    d = w_ref.shape[1]
    for sub in range(x_ref.shape[0] // SUB_ROWS):
        rows = slice(sub * SUB_ROWS, (sub + 1) * SUB_ROWS)
        y_buf = y_ref.at[sub % 2]
        _out0_prologue(of_ref, ob_ref, ag_ref, yb_ref, og_ref, rows, y_buf)
        for c0 in range(0, d, COL_CHUNK):
            cs = slice(c0, c0 + COL_CHUNK)
            o_ref[rows, cs] = x_ref[rows, cs] + _dot(y_buf[...], w_ref[:, cs])
        _rmsnorm_rows(o_ref, g1_ref, h1_ref.at[rows], sub * SUB_ROWS, SUB_ROWS)


def _out0(o_f, o_b, p2, yb, og, w_out, x2, g1, tm):
    t, d = x2.shape
    aw = A_HEADS * A_DV
    kdim = aw + B_W
    return pl.pallas_call(
        _out0_kernel,
        grid=(t // tm,),
        in_specs=[
            pl.BlockSpec((tm, aw), lambda i: (i, 0)),
            pl.BlockSpec((tm, aw), lambda i: (i, 0)),
            pl.BlockSpec((tm, aw), lambda i: (i, COL_AGATE // aw)),
            pl.BlockSpec((tm, B_W), lambda i: (i, 0)),
            _resident((1, A_DV)),
            _resident((kdim, d)),
            pl.BlockSpec((tm, d), lambda i: (i, 0)),
            _resident((1, d)),
        ],
        out_specs=[pl.BlockSpec((tm, d), lambda i: (i, 0)), pl.BlockSpec((tm, d), lambda i: (i, 0))],
        out_shape=[jax.ShapeDtypeStruct((t, d), F32), jax.ShapeDtypeStruct((t, d), BF16)],
        scratch_shapes=[pltpu.VMEM((2, SUB_ROWS, kdim), BF16)],
        compiler_params=_params(("arbitrary",)),
    )(o_f, o_b, p2, yb, og, w_out, x2, g1)


def _gelu(x):
    return 0.5 * x * (1.0 + lax.erf(x * (2.0 ** -0.5)))


def _proj1_kernel(h_ref, w_ref, p_ref):
    n = w_ref.shape[1]
    for sub in range(h_ref.shape[0] // SUB_ROWS):
        rows = slice(sub * SUB_ROWS, (sub + 1) * SUB_ROWS)
        for c0 in range(0, n, COL_CHUNK):
            cs = slice(c0, c0 + COL_CHUNK)
            y = _dot(h_ref[rows, :], w_ref[:, cs])
            act = _gelu(y) if c0 < 2 * n // 3 else y * jax.nn.sigmoid(y)
            p_ref[rows, cs] = act.astype(BF16)


def _proj1(h2, w, tm):
    t, d = h2.shape
    n = w.shape[1]
    return pl.pallas_call(
        _proj1_kernel,
        grid=(t // tm,),
        in_specs=[pl.BlockSpec((tm, d), lambda i: (i, 0)), _resident((d, n))],
        out_specs=pl.BlockSpec((tm, n), lambda i: (i, 0)),
        out_shape=jax.ShapeDtypeStruct((t, n), BF16),
        compiler_params=_params(("arbitrary",)),
    )(h2, w)


def _out1_prologue(u_ref, v_ref, gt_ref, lng_ref, lnb_ref, ws_ref, bst_ref, row0, z_buf):
    gw = v_ref.shape[1] // C_GROUPS
    for c in range(SUB_ROWS // C_CHUNK):
        rows = slice(row0 + c * C_CHUNK, row0 + (c + 1) * C_CHUNK)
        zrows = slice(c * C_CHUNK, (c + 1) * C_CHUNK)
        v = v_ref[rows, :].astype(F32)
        mu = jnp.mean(v, axis=-1, keepdims=True)
        vc = v - mu
        var = jnp.mean(vc * vc, axis=-1, keepdims=True)
        vn = (vc * lax.rsqrt(var + EPS) * lng_ref[...] + lnb_ref[...]).astype(BF16)
        for g in range(C_GROUPS):
            gs = slice(g * gw, (g + 1) * gw)
            s = _dot(ws_ref[g], vn[:, gs]) + bst_ref[:, g:g + 1]
            z = u_ref[rows, gs].astype(F32) * s * gt_ref[rows, gs].astype(F32)
            z_buf[zrows, gs] = z.astype(BF16)


def _out1_kernel(u_ref, v_ref, gt_ref, un_ref, vn_ref, gtn_ref, lng_ref, lnb_ref, ws_ref, bst_ref, w_ref,
                 x_ref, o_ref, z_ref):
    d = w_ref.shape[1]
    n_sub = x_ref.shape[0] // SUB_ROWS
    params = (lng_ref, lnb_ref, ws_ref, bst_ref)

    @pl.when(pl.program_id(0) == 0)
    def _():
        _out1_prologue(u_ref, v_ref, gt_ref, *params, 0, z_ref.at[0])

    for sub in range(n_sub):
        rows = slice(sub * SUB_ROWS, (sub + 1) * SUB_ROWS)
        z_buf = z_ref.at[sub % 2]
        for c0 in range(0, d, COL_CHUNK):
            cs = slice(c0, c0 + COL_CHUNK)
            o_ref[rows, cs] = x_ref[rows, cs] + _dot(z_buf[...], w_ref[:, cs])
        if sub + 1 < n_sub:
            _out1_prologue(u_ref, v_ref, gt_ref, *params, (sub + 1) * SUB_ROWS, z_ref.at[(sub + 1) % 2])
        else:
            _out1_prologue(un_ref, vn_ref, gtn_ref, *params, 0, z_ref.at[0])


def _out1(p2, lng, lnb, ws, bst, w_out, x2, tm):
    t, d = x2.shape
    cw = w_out.shape[0]
    assert (tm // SUB_ROWS) % 2 == 0
    return pl.pallas_call(
        _out1_kernel,
        grid=(t // tm,),
        in_specs=[
            pl.BlockSpec((tm, cw), lambda i: (i, 0)),
            pl.BlockSpec((tm, cw), lambda i: (i, 1)),
            pl.BlockSpec((tm, cw), lambda i: (i, 2)),
            _next_head_spec(tm, cw, 0, t),
            _next_head_spec(tm, cw, 1, t),
            _next_head_spec(tm, cw, 2, t),
            _resident((1, cw)),
            _resident((1, cw)),
            _resident((C_GROUPS, C_CHUNK, C_CHUNK)),
            _resident((C_CHUNK, C_GROUPS)),
            _resident((cw, d)),
            pl.BlockSpec((tm, d), lambda i: (i, 0)),
        ],
        out_specs=pl.BlockSpec((tm, d), lambda i: (i, 0)),
        out_shape=jax.ShapeDtypeStruct((t, d), F32),
        scratch_shapes=[pltpu.VMEM((2, SUB_ROWS, cw), BF16)],
        compiler_params=_params(("arbitrary",)),
    )(p2, p2, p2, p2, p2, p2, lng, lnb, ws, bst, w_out, x2)


def _rope_tables(l):
    t = jnp.arange(l)
    row = (t // GRID_W).astype(F32)
    col = (t % GRID_W).astype(F32)
    n_freq = B_HD // 4
    inv = ROPE_THETA ** (-jnp.arange(n_freq, dtype=F32) / n_freq)
    ar = row[:, None] * inv
    ac = col[:, None] * inv
    cos = jnp.concatenate([jnp.cos(ar), jnp.cos(ar), jnp.cos(ac), jnp.cos(ac)], axis=-1)
    sin = jnp.concatenate([-jnp.sin(ar), jnp.sin(ar), -jnp.sin(ac), jnp.sin(ac)], axis=-1)
    return cos, sin


def _lane_row(v):
    v = v.reshape(1, -1).astype(F32)
    return jnp.pad(v, ((0, 0), (0, LANES - v.shape[1])))


def _pick(n, pref):
    while n % pref:
        pref //= 2
    return pref


def _hybrid_layer(x2, b, l, norm_g, w_in, conv_w, a_log, dt_bias, onorm_g, qn_g, kn_g, w_out, next_norm_g):
    t, d = x2.shape
    sizes = (A_W, A_W, A_HEADS * A_DV, A_HEADS * A_DV, 4 * A_HEADS, B_W, B_KV_W, B_KV_W, B_W)
    offs = [0]
    for s in sizes:
        offs.append(offs[-1] + s)
    aq, ak, av, agate, alog_in, bq, bk, bv, bgate = [w_in[:, offs[i]:offs[i + 1]] for i in range(9)]
    w_main = jnp.concatenate([aq, ak, av, agate, bq, bgate, bk, bv], axis=1).astype(BF16)
    w_log = jnp.pad(alog_in, ((0, 0), (0, LANES - alog_in.shape[1]))).astype(BF16)

    tm = _pick(math.gcd(t, l), 512)
    cos, sin_signed = _rope_tables(l)
    p2, lg2 = _proj0(x2, norm_g.reshape(1, d), w_main, w_log, cos, sin_signed,
                     qn_g.reshape(1, B_HD).astype(F32), kn_g.reshape(1, B_HD).astype(F32), tm)
    p3 = p2.reshape(b, l, P0_COLS)
    lg3 = lg2.reshape(b, l, LANES)

    tb = _pick(l, 512)
    q3, k3, v3, gc3 = _prep(p3, lg3, conv_w.astype(F32), _lane_row(a_log), _lane_row(dt_bias), tb)
    o_f, o_b = _scan(q3, k3, v3, gc3, tb, 2 if b % 2 == 0 else 1)

    yb = _attn(p3, _pick(l, 512), _pick(l, 512))

    return _out0(o_f.reshape(t, -1), o_b.reshape(t, -1), p2, yb.reshape(t, B_W),
                 onorm_g.reshape(1, A_DV).astype(F32), w_out.astype(BF16), x2,
                 next_norm_g.reshape(1, d).astype(F32), tm)


def _gmlp_layer(x2, h2, w_in, ln_g, ln_b, w_s, b_s, w_out):
    t, d = x2.shape
    cw = w_out.shape[0]
    tm = _pick(t, 512)
    p2 = _proj1(h2, w_in.astype(BF16), tm)
    return _out1(p2, ln_g.reshape(1, cw).astype(F32), ln_b.reshape(1, cw).astype(F32),
                 w_s.astype(BF16), jnp.transpose(b_s).astype(F32), w_out.astype(BF16), x2, tm)


def kernel(x, norm0_g, w_in0, conv0_w, a_log0, dt_bias0, a_onorm_g0, b_qnorm_g0, b_knorm_g0, w_out0,
           norm1_g, w_in1, c_ln_g1, c_ln_b1, c_ws1, c_bs1, w_out1):
    b, l, d = x.shape
    assert w_in0.shape[0] == 1 and w_in1.shape[0] == 1
    x2 = x.reshape(b * l, d)
    x2, h2 = _hybrid_layer(x2, b, l, norm0_g[0], w_in0[0], conv0_w[0], a_log0[0], dt_bias0[0],
                           a_onorm_g0[0], b_qnorm_g0[0], b_knorm_g0[0], w_out0[0], norm1_g[0])
    x2 = _gmlp_layer(x2, h2, w_in1[0], c_ln_g1[0], c_ln_b1[0], c_ws1[0], c_bs1[0], w_out1[0])
    return x2.reshape(b, l, d)
```
